```python
import jax, jax.numpy as jnp
from jax import lax
import numpy as np

D_MODEL = 1024
BATCH = 2
SEQ = 8192
DEPTH = 2
DEC_BATCH = 4
DEC_SEQ = 8192
PAST_LEN = 128

CONV_DIM = 512
CONV_WIDTH = 31
N_HEADS = 8
QK_NOPE = 64
QK_ROPE = 32
V_HEAD = 64
Q_LORA = 384
KV_LORA = 256
ROPE_THETA = 10000.0
Q_BLOCK = 128
SOFTMAX_SCALE = (QK_NOPE + QK_ROPE) ** -0.5
N_GROUPS = 4
EXPERTS_PER_GROUP = 8
N_EXPERTS = N_GROUPS * EXPERTS_PER_GROUP
TOP_K = 2
D_EXPERT = 256
DEEPNORM_ALPHA = (2 * DEPTH) ** 0.25
DEEPNORM_BETA = (8 * DEPTH) ** -0.25
LN_EPS = 1e-5
SPLITS = (2 * CONV_DIM, 2 * CONV_DIM + Q_LORA, 2 * CONV_DIM + Q_LORA + KV_LORA,
          2 * CONV_DIM + Q_LORA + KV_LORA + QK_ROPE)
IN_COLS = SPLITS[-1] + 2 * D_MODEL

kernel_name = 'hybrid_conformer_mla_hmoe_encoder'


def layer_norm(x, g, b):
    xf = x.astype(jnp.float32)
    mu = jnp.mean(xf, axis=-1, keepdims=True)
    var = jnp.mean(jnp.square(xf - mu), axis=-1, keepdims=True)
    y = (xf - mu) * lax.rsqrt(var + LN_EPS)
    return (y * g.astype(jnp.float32) + b.astype(jnp.float32)).astype(x.dtype)


def rms_norm(x, g):
    xf = x.astype(jnp.float32)
    y = xf * lax.rsqrt(jnp.mean(jnp.square(xf), axis=-1, keepdims=True) + LN_EPS)
    return (y * g.astype(jnp.float32)).astype(x.dtype)


def rope_tables(seq_len):
    inv_freq = ROPE_THETA ** (-jnp.arange(0, QK_ROPE, 2, dtype=jnp.float32) / QK_ROPE)
    ang = jnp.arange(seq_len, dtype=jnp.float32)[:, None] * inv_freq[None, :]
    return jnp.cos(ang), jnp.sin(ang)


def apply_rope(x, cos, sin):
    c = cos.astype(x.dtype)
    s = sin.astype(x.dtype)
    x1, x2 = jnp.split(x, 2, axis=-1)
    return jnp.concatenate([x1 * c - x2 * s, x2 * c + x1 * s], axis=-1)


def mla_attention(q_nope, q_rope, k_nope, k_rope, v):
    B, S, H, _ = q_nope.shape
    nb = S // Q_BLOCK

    def to_blocks(t):
        return jnp.moveaxis(t.reshape((B, nb, Q_BLOCK) + t.shape[2:]), 1, 0)

    def block(qs):
        qn, qr = qs
        s = jnp.einsum('bqhd,bkhd->bhqk', qn, k_nope) + jnp.einsum('bqhr,bkr->bhqk', qr, k_rope)
        p = jax.nn.softmax(s.astype(jnp.float32) * SOFTMAX_SCALE, axis=-1).astype(v.dtype)
        return jnp.einsum('bhqk,bkhd->bqhd', p, v)

    o = lax.map(block, (to_blocks(q_nope), to_blocks(q_rope)))
    return jnp.moveaxis(o, 0, 1).reshape(B, S, H * V_HEAD)


def token_mixer(x, w_in, b_in, dw_kernel, dw_bias, conv_ln_g, conv_ln_b, w_conv_out,
                q_norm_g, w_q_b, kv_norm_g, w_kv_b, w_mla_out, w_out):
    B, S, _ = x.shape
    h = x @ w_in + b_in
    glu_in, q_a, kv_a, k_rope, gate_in = jnp.split(h, SPLITS, axis=-1)
    a, g = jnp.split(glu_in, 2, axis=-1)
    u = a * jax.nn.sigmoid(g)
    pad = CONV_WIDTH // 2
    u = lax.conv_general_dilated(u, dw_kernel[:, None, :], window_strides=(1,), padding=[(pad, pad)],
                                 dimension_numbers=('NWC', 'WIO', 'NWC'),
                                 feature_group_count=CONV_DIM) + dw_bias
    u = jax.nn.silu(layer_norm(u, conv_ln_g, conv_ln_b))
    conv_out = u @ w_conv_out
    q = (rms_norm(q_a, q_norm_g) @ w_q_b).reshape(B, S, N_HEADS, QK_NOPE + QK_ROPE)
    q_nope, q_rope = jnp.split(q, [QK_NOPE], axis=-1)
    kv = (rms_norm(kv_a, kv_norm_g) @ w_kv_b).reshape(B, S, N_HEADS, QK_NOPE + V_HEAD)
    k_nope, v = jnp.split(kv, [QK_NOPE], axis=-1)
    cos, sin = rope_tables(S)
    q_rope = apply_rope(q_rope, cos[:, None, :], sin[:, None, :])
    k_rope = apply_rope(k_rope, cos, sin)
    mla_out = mla_attention(q_nope, q_rope, k_nope, k_rope, v) @ w_mla_out
    g_conv, g_mla = jnp.split(jax.nn.sigmoid(gate_in), 2, axis=-1)
    return (g_conv * conv_out + g_mla * mla_out) @ w_out


def hier_moe(x, w_router_group, b_router_group, w_router_expert, b_router_expert,
             w_gate_e, w_up_e, w_down_e):
    B, S, D = x.shape
    t = x.reshape(-1, D)
    T = t.shape[0]
    gp = jax.nn.softmax((t @ w_router_group + b_router_group).astype(jnp.float32), axis=-1)
    g_w, g_idx = lax.top_k(gp, 1)
    el = (t @ w_router_expert + b_router_expert).astype(jnp.float32).reshape(T, N_GROUPS, EXPERTS_PER_GROUP)
    el_sel = jnp.take_along_axis(el, g_idx[:, :, None], axis=1)[:, 0]
    e_w, e_idx = lax.top_k(jax.nn.softmax(el_sel, axis=-1), TOP_K)
    e_w = e_w / jnp.sum(e_w, axis=-1, keepdims=True)
    weights = g_w * e_w
    expert_id = g_idx * EXPERTS_PER_GROUP + e_idx
    combine = jnp.sum(jax.nn.one_hot(expert_id, N_EXPERTS, dtype=jnp.float32) * weights[..., None],
                      axis=1).astype(x.dtype)

    def expert_step(acc, xs):
        wg, wu, wd, c = xs
        hdn = jax.nn.silu(t @ wg) * (t @ wu)
        return acc + c[:, None] * (hdn @ wd), None

    acc, _ = lax.scan(expert_step, jnp.zeros_like(t), (w_gate_e, w_up_e, w_down_e, combine.T))
    return acc.reshape(B, S, D)


def trunk(x, ln_in_g, ln_in_b, w_in, b_in, dw_kernel, dw_bias, conv_ln_g, conv_ln_b, w_conv_out,
          q_norm_g, w_q_b, kv_norm_g, w_kv_b, w_mla_out, w_out, ln1_g, ln1_b,
          w_router_group, b_router_group, w_router_expert, b_router_expert,
          w_gate_e, w_up_e, w_down_e, ln2_g, ln2_b):
    x = layer_norm(x, ln_in_g, ln_in_b)
    for l in range(DEPTH):
        m = token_mixer(x, w_in[l], b_in[l], dw_kernel[l], dw_bias[l], conv_ln_g[l], conv_ln_b[l],
                        w_conv_out[l], q_norm_g[l], w_q_b[l], kv_norm_g[l], w_kv_b[l],
                        w_mla_out[l], w_out[l])
        x = layer_norm(DEEPNORM_ALPHA * x + m, ln1_g[l], ln1_b[l])
        f = hier_moe(x, w_router_group[l], b_router_group[l], w_router_expert[l], b_router_expert[l],
                     w_gate_e[l], w_up_e[l], w_down_e[l])
        x = layer_norm(DEEPNORM_ALPHA * x + f, ln2_g[l], ln2_b[l])
    return x


def setup_inputs(seed: int = 0) -> dict:
    key = jax.random.key(seed)
    ks = jax.random.split(key, 32)

    def nrm(k, shape, scale):
        return jax.random.normal(k, shape, dtype=jnp.float32) * scale

    def gain(k, shape):
        return 1.0 + nrm(k, shape, 0.02)

    L = DEPTH
    beta = DEEPNORM_BETA
    return {
        'x_prompt': nrm(ks[0], (BATCH, SEQ, D_MODEL), 1.0),
        'x_sample': nrm(ks[1], (DEC_BATCH, DEC_SEQ, D_MODEL), 1.0),
        'ln_in_g': gain(ks[2], (D_MODEL,)),
        'ln_in_b': nrm(ks[3], (D_MODEL,), 0.02),
        'w_in': nrm(ks[4], (L, D_MODEL, IN_COLS), D_MODEL ** -0.5),
        'b_in': nrm(ks[5], (L, IN_COLS), 0.02),
        'dw_kernel': nrm(ks[6], (L, CONV_WIDTH, CONV_DIM), CONV_WIDTH ** -0.5),
        'dw_bias': nrm(ks[7], (L, CONV_DIM), 0.02),
        'conv_ln_g': gain(ks[8], (L, CONV_DIM)),
        'conv_ln_b': nrm(ks[9], (L, CONV_DIM), 0.02),
        'w_conv_out': nrm(ks[10], (L, CONV_DIM, D_MODEL), beta * CONV_DIM ** -0.5),
        'q_norm_g': gain(ks[11], (L, Q_LORA)),
        'w_q_b': nrm(ks[12], (L, Q_LORA, N_HEADS * (QK_NOPE + QK_ROPE)), Q_LORA ** -0.5),
        'kv_norm_g': gain(ks[13], (L, KV_LORA)),
        'w_kv_b': nrm(ks[14], (L, KV_LORA, N_HEADS * (QK_NOPE + V_HEAD)), KV_LORA ** -0.5),
        'w_mla_out': nrm(ks[15], (L, N_HEADS * V_HEAD, D_MODEL), beta * (N_HEADS * V_HEAD) ** -0.5),
        'w_out': nrm(ks[16], (L, D_MODEL, D_MODEL), beta * D_MODEL ** -0.5),
        'ln1_g': gain(ks[17], (L, D_MODEL)),
        'ln1_b': nrm(ks[18], (L, D_MODEL), 0.02),
        'w_router_group': nrm(ks[19], (L, D_MODEL, N_GROUPS), D_MODEL ** -0.5),
        'b_router_group': nrm(ks[20], (L, N_GROUPS), 0.01),
        'w_router_expert': nrm(ks[21], (L, D_MODEL, N_EXPERTS), D_MODEL ** -0.5),
        'b_router_expert': nrm(ks[22], (L, N_EXPERTS), 0.01),
        'w_gate_e': nrm(ks[23], (L, N_EXPERTS, D_MODEL, D_EXPERT), D_MODEL ** -0.5),
        'w_up_e': nrm(ks[24], (L, N_EXPERTS, D_MODEL, D_EXPERT), beta * D_MODEL ** -0.5),
        'w_down_e': nrm(ks[25], (L, N_EXPERTS, D_EXPERT, D_MODEL), beta * D_EXPERT ** -0.5),
        'ln2_g': gain(ks[26], (L, D_MODEL)),
        'ln2_b': nrm(ks[27], (L, D_MODEL), 0.02),
    }


def reference(x_prompt, x_sample, ln_in_g, ln_in_b, w_in, b_in, dw_kernel, dw_bias, conv_ln_g, conv_ln_b,
              w_conv_out, q_norm_g, w_q_b, kv_norm_g, w_kv_b, w_mla_out, w_out, ln1_g, ln1_b,
              w_router_group, b_router_group, w_router_expert, b_router_expert,
              w_gate_e, w_up_e, w_down_e, ln2_g, ln2_b):
    params = (ln_in_g, ln_in_b, w_in, b_in, dw_kernel, dw_bias, conv_ln_g, conv_ln_b, w_conv_out,
              q_norm_g, w_q_b, kv_norm_g, w_kv_b, w_mla_out, w_out, ln1_g, ln1_b,
              w_router_group, b_router_group, w_router_expert, b_router_expert,
              w_gate_e, w_up_e, w_down_e, ln2_g, ln2_b)
    y_prompt = trunk(x_prompt, *params)
    y_sample = trunk(x_sample, *params)
    return (y_prompt, y_sample)
```

```python
import functools
import math

import jax
import jax.numpy as jnp
from jax import lax
from jax.experimental import pallas as pl
from jax.experimental.pallas import tpu as pltpu

D_MODEL = 1024
DEPTH = 2
CONV_DIM = 512
CONV_WIDTH = 31
N_HEADS = 8
QK_NOPE = 64
QK_ROPE = 32
V_HEAD = 64
Q_LORA = 384
KV_LORA = 256
ROPE_THETA = 10000.0
SOFTMAX_SCALE = (QK_NOPE + QK_ROPE) ** -0.5
N_GROUPS = 4
EXPERTS_PER_GROUP = 8
N_EXPERTS = N_GROUPS * EXPERTS_PER_GROUP
D_EXPERT = 256
DEEPNORM_ALPHA = (2 * DEPTH) ** 0.25
LN_EPS = 1e-5

LANES = 128
HEAD_PAD = LANES
HALF_ROPE = QK_ROPE // 2
HALO = 16
GLU_COLS = 2 * CONV_DIM
PROJ_COLS = GLU_COLS + Q_LORA + KV_LORA + HEAD_PAD
VMEM_LIMIT = 56 * 1024 * 1024

BF16 = jnp.bfloat16
F32 = jnp.float32


def _ln(x, g, b):
    mu = jnp.mean(x, axis=-1, keepdims=True)
    xc = x - mu
    var = jnp.mean(xc * xc, axis=-1, keepdims=True)
    return xc * lax.rsqrt(var + LN_EPS) * g + b


def _rms(x, g):
    return x * lax.rsqrt(jnp.mean(x * x, axis=-1, keepdims=True) + LN_EPS) * g


def _dot(a, b):
    return jnp.dot(a, b, preferred_element_type=F32)


def _rope(x, c, s_fwd, s_bwd):
    return x * c + pltpu.roll(x, HALF_ROPE, 1) * s_fwd + pltpu.roll(x, HEAD_PAD - HALF_ROPE, 1) * s_bwd


def _ln_kernel(x_ref, g_ref, b_ref, o_ref):
    o_ref[...] = _ln(x_ref[...], g_ref[...], b_ref[...])


def _layer_norm_call(x, g, b, tm):
    T = x.shape[0]
    row = lambda i: (i, 0)
    const = lambda i: (0, 0)
    return pl.pallas_call(
        _ln_kernel,
        grid=(T // tm,),
        in_specs=[pl.BlockSpec((tm, D_MODEL), row), pl.BlockSpec((1, D_MODEL), const),
                  pl.BlockSpec((1, D_MODEL), const)],
        out_specs=pl.BlockSpec((tm, D_MODEL), row),
        out_shape=jax.ShapeDtypeStruct((T, D_MODEL), F32),
        compiler_params=pltpu.CompilerParams(dimension_semantics=("parallel",), vmem_limit_bytes=VMEM_LIMIT),
    )(x, g, b)


def _proj_kernel(x_ref, w1_ref, b1_ref, qg_ref, wq_ref, kvg_ref, wk_ref, wv_ref, vone_ref,
                 cq_ref, sfq_ref, sbq_ref, ck_ref, sfk_ref, sbk_ref,
                 u_ref, q_ref, k_ref, v_ref):
    x = x_ref[...].astype(BF16)
    h = _dot(x, w1_ref[...]) + b1_ref[...]
    a = h[:, :CONV_DIM]
    g = h[:, CONV_DIM:GLU_COLS]
    u_ref[...] = a * jax.nn.sigmoid(g)
    qa = h[:, GLU_COLS:GLU_COLS + Q_LORA]
    kva = h[:, GLU_COLS + Q_LORA:GLU_COLS + Q_LORA + KV_LORA]
    kr = h[:, GLU_COLS + Q_LORA + KV_LORA:]

    q = _dot(_rms(qa, qg_ref[...]).astype(BF16), wq_ref[...])
    cq, sfq, sbq = cq_ref[...], sfq_ref[...], sbq_ref[...]
    for hd in range(N_HEADS):
        sl = slice(hd * HEAD_PAD, (hd + 1) * HEAD_PAD)
        q_ref[:, sl] = _rope(q[:, sl], cq, sfq, sbq).astype(BF16)

    kvn = _rms(kva, kvg_ref[...]).astype(BF16)
    kn = _dot(kvn, wk_ref[...])
    krr = _rope(kr, ck_ref[...], sfk_ref[...], sbk_ref[...])
    for hd in range(N_HEADS):
        sl = slice(hd * HEAD_PAD, (hd + 1) * HEAD_PAD)
        k_ref[:, sl] = (kn[:, sl] + krr).astype(BF16)
    v_ref[...] = (_dot(kvn, wv_ref[...]) + vone_ref[...]).astype(BF16)


def _proj_call(x, lw, tabs, S, tm):
    T = x.shape[0]
    nper = S // tm
    row = lambda i: (i, 0)
    const = lambda i: (0, 0)
    pos = lambda i: (i % nper, 0)
    full = lambda a: pl.BlockSpec(a.shape, const)
    tab_spec = pl.BlockSpec((tm, HEAD_PAD), pos)
    wide = D_MODEL
    return pl.pallas_call(
        _proj_kernel,
        grid=(T // tm,),
        in_specs=[pl.BlockSpec((tm, D_MODEL), row), full(lw['w1']), full(lw['b1']), full(lw['qg']),
                  full(lw['wq']), full(lw['kvg']), full(lw['wk']), full(lw['wv']), full(lw['vone'])]
                 + [tab_spec] * 6,
        out_specs=[pl.BlockSpec((tm, CONV_DIM), row), pl.BlockSpec((tm, wide), row),
                   pl.BlockSpec((tm, wide), row), pl.BlockSpec((tm, wide), row)],
        out_shape=[jax.ShapeDtypeStruct((T, CONV_DIM), F32), jax.ShapeDtypeStruct((T, wide), BF16),
                   jax.ShapeDtypeStruct((T, wide), BF16), jax.ShapeDtypeStruct((T, wide), BF16)],
        compiler_params=pltpu.CompilerParams(dimension_semantics=("parallel",), vmem_limit_bytes=VMEM_LIMIT),
    )(x, lw['w1'], lw['b1'], lw['qg'], lw['wq'], lw['kvg'], lw['wk'], lw['wv'], lw['vone'], *tabs)


def _attn_kernel(q_ref, k_ref, v_ref, o_ref, *, tk):
    tq = q_ref.shape[0]
    S = k_ref.shape[0]
    accs = []
    for hh in range(2):
        sl = slice(hh * HEAD_PAD, (hh + 1) * HEAD_PAD)
        q = q_ref[:, sl]

        def body(j, carry, sl=sl, q=q):
            m, acc = carry
            rows = pl.ds(pl.multiple_of(j * tk, tk), tk)
            s = lax.dot_general(q, k_ref[rows, sl], (((1,), (1,)), ((), ())), preferred_element_type=F32)
            m_new = jnp.maximum(m, jnp.max(s, axis=1, keepdims=True))
            p = jnp.exp2(s - m_new).astype(BF16)
            acc = jnp.exp2(m - m_new) * acc + _dot(p, v_ref[rows, sl])
            return m_new, acc

        _, acc = lax.fori_loop(0, S // tk, body,
                               (jnp.full((tq, 1), -jnp.inf, F32), jnp.zeros((tq, HEAD_PAD), F32)))
        accs.append(acc)
    lane = lax.broadcasted_iota(jnp.int32, (tq, HEAD_PAD), 1)
    even = accs[0] / accs[0][:, V_HEAD:V_HEAD + 1]
    odd = accs[1] / accs[1][:, 0:1]
    o_ref[...] = jnp.where(lane < V_HEAD, even, odd).astype(BF16)


def _attn_call(q, k, v, S, tq, tk):
    T = q.shape[0]
    nb = T // S
    nq = S // tq
    pair = 2 * HEAD_PAD
    return pl.pallas_call(
        functools.partial(_attn_kernel, tk=tk),
        grid=(nb, N_HEADS // 2, nq),
        in_specs=[pl.BlockSpec((tq, pair), lambda b, hp, i: (b * nq + i, hp)),
                  pl.BlockSpec((S, pair), lambda b, hp, i: (b, hp)),
                  pl.BlockSpec((S, pair), lambda b, hp, i: (b, hp))],
        out_specs=pl.BlockSpec((tq, 2 * V_HEAD), lambda b, hp, i: (b * nq + i, hp)),
        out_shape=jax.ShapeDtypeStruct((T, N_HEADS * V_HEAD), BF16),
        compiler_params=pltpu.CompilerParams(dimension_semantics=("parallel", "parallel", "parallel"),
                                             vmem_limit_bytes=VMEM_LIMIT),
    )(q, k, v)


def _route(logits):
    lane = lax.broadcasted_iota(jnp.int32, logits.shape, 1)
    neg = -jnp.inf
    is_group = (lane >= N_EXPERTS) & (lane < N_EXPERTS + N_GROUPS)
    gl = jnp.where(is_group, logits, neg)
    gmax = jnp.max(gl, axis=1, keepdims=True)
    g_w = 1.0 / jnp.sum(jnp.exp(gl - gmax), axis=1, keepdims=True)
    g_idx = jnp.min(jnp.where(gl == gmax, lane, 2 * LANES), axis=1, keepdims=True) - N_EXPERTS
    lo = g_idx * EXPERTS_PER_GROUP
    el = jnp.where((lane >= lo) & (lane < lo + EXPERTS_PER_GROUP), logits, neg)
    m1 = jnp.max(el, axis=1, keepdims=True)
    i1 = jnp.min(jnp.where(el == m1, lane, 2 * LANES), axis=1, keepdims=True)
    el2 = jnp.where(lane == i1, neg, el)
    m2 = jnp.max(el2, axis=1, keepdims=True)
    i2 = jnp.min(jnp.where(el2 == m2, lane, 2 * LANES), axis=1, keepdims=True)
    r = jnp.exp(m2 - m1)
    w1 = g_w / (1.0 + r)
    w2 = g_w * r / (1.0 + r)
    return i1, i2, w1, w2, lane


def _post_kernel(x_ref, u_ref, up_ref, un_ref, o_ref,
                 dwk_ref, dwb_ref, clg_ref, clb_ref, wco_ref, wg_ref, bg_ref, wmo_ref, wo_ref,
                 l1g_ref, l1b_ref, wrh_ref, wrl_ref, br_ref,
                 x1_ref, comb_ref, ubuf, *, nper):
    tm = x_ref.shape[0]
    i = pl.program_id(0)
    first = (i % nper) == 0
    last = (i % nper) == nper - 1
    ubuf[0:HALO, :] = jnp.where(first, 0.0, up_ref[...])
    ubuf[HALO:HALO + tm, :] = u_ref[...]
    ubuf[HALO + tm:, :] = jnp.where(last, 0.0, un_ref[...])
    conv = jnp.zeros((tm, CONV_DIM), F32) + dwb_ref[...]
    off = HALO - CONV_WIDTH // 2
    for w in range(CONV_WIDTH):
        conv = conv + ubuf[off + w:off + w + tm, :] * dwk_ref[w:w + 1, :]
    c = _ln(conv, clg_ref[...], clb_ref[...])
    c = c * jax.nn.sigmoid(c)
    conv_out = _dot(c.astype(BF16), wco_ref[...])

    x = x_ref[...]
    gates = jax.nn.sigmoid(_dot(x.astype(BF16), wg_ref[...]) + bg_ref[...])
    mla_out = _dot(o_ref[...], wmo_ref[...])
    merged = gates[:, :D_MODEL] * conv_out + gates[:, D_MODEL:] * mla_out
    m = _dot(merged.astype(BF16), wo_ref[...])
    x1 = _ln(DEEPNORM_ALPHA * x + m, l1g_ref[...], l1b_ref[...])
    x1_ref[...] = x1

    hi = x1.astype(BF16)
    lo = (x1 - hi.astype(F32)).astype(BF16)
    logits = _dot(hi, wrh_ref[...]) + _dot(lo, wrh_ref[...]) + _dot(hi, wrl_ref[...]) + br_ref[...]
    i1, i2, w1, w2, lane = _route(logits)
    comb_ref[...] = jnp.where(lane == i1, w1, 0.0) + jnp.where(lane == i2, w2, 0.0)


def _post_call(x, u, o, lw, S, tm):
    T = x.shape[0]
    nper = S // tm
    nh = tm // HALO
    last_halo = T // HALO - 1
    row = lambda i: (i, 0)
    const = lambda i: (0, 0)
    full = lambda a: pl.BlockSpec(a.shape, const)
    names = ['dwk', 'dwb', 'clg', 'clb', 'wco', 'wg', 'bg', 'wmo', 'wo', 'l1g', 'l1b', 'wrh', 'wrl', 'br']
    ws = [lw[n] for n in names]
    return pl.pallas_call(
        functools.partial(_post_kernel, nper=nper),
        grid=(T // tm,),
        in_specs=[pl.BlockSpec((tm, D_MODEL), row), pl.BlockSpec((tm, CONV_DIM), row),
                  pl.BlockSpec((HALO, CONV_DIM), lambda i: (jnp.maximum(i * nh - 1, 0), 0)),
                  pl.BlockSpec((HALO, CONV_DIM), lambda i: (jnp.minimum((i + 1) * nh, last_halo), 0)),
                  pl.BlockSpec((tm, N_HEADS * V_HEAD), row)] + [full(a) for a in ws],
        out_specs=[pl.BlockSpec((tm, D_MODEL), row), pl.BlockSpec((tm, LANES), row)],
        out_shape=[jax.ShapeDtypeStruct((T, D_MODEL), F32), jax.ShapeDtypeStruct((T, LANES), F32)],
        scratch_shapes=[pltpu.VMEM((tm + 2 * HALO, CONV_DIM), F32)],
        compiler_params=pltpu.CompilerParams(dimension_semantics=("parallel",), vmem_limit_bytes=VMEM_LIMIT),
    )(x, u, u, u, o, *ws)


def _moe_kernel(x_ref, comb_ref, wg_ref, wu_ref, wd_ref, l2g_ref, l2b_ref, o_ref, acc_ref):
    e = pl.program_id(1)

    @pl.when(e == 0)
    def _():
        acc_ref[...] = jnp.zeros_like(acc_ref)

    xb = x_ref[...].astype(BF16)
    gate = _dot(xb, wg_ref[0])
    hdn = gate * jax.nn.sigmoid(gate) * _dot(xb, wu_ref[0])
    y = _dot(hdn.astype(BF16), wd_ref[0])
    comb = comb_ref[...]
    lane = lax.broadcasted_iota(jnp.int32, comb.shape, 1)
    c = jnp.sum(jnp.where(lane == e, comb, 0.0), axis=1, keepdims=True)
    acc_ref[...] += c * y

    @pl.when(e == N_EXPERTS - 1)
    def _():
        o_ref[...] = _ln(DEEPNORM_ALPHA * x_ref[...] + acc_ref[...], l2g_ref[...], l2b_ref[...])


def _moe_call(x, comb, lw, tm):
    T = x.shape[0]
    row = lambda i, e: (i, 0)
    const = lambda i, e: (0, 0)
    ex = lambda i, e: (e, 0, 0)
    return pl.pallas_call(
        _moe_kernel,
        grid=(T // tm, N_EXPERTS),
        in_specs=[pl.BlockSpec((tm, D_MODEL), row), pl.BlockSpec((tm, LANES), row),
                  pl.BlockSpec((1, D_MODEL, D_EXPERT), ex), pl.BlockSpec((1, D_MODEL, D_EXPERT), ex),
                  pl.BlockSpec((1, D_EXPERT, D_MODEL), ex),
                  pl.BlockSpec((1, D_MODEL), const), pl.BlockSpec((1, D_MODEL), const)],
        out_specs=pl.BlockSpec((tm, D_MODEL), row),
        out_shape=jax.ShapeDtypeStruct((T, D_MODEL), F32),
        scratch_shapes=[pltpu.VMEM((tm, D_MODEL), F32)],
        compiler_params=pltpu.CompilerParams(dimension_semantics=("parallel", "arbitrary"),
                                             vmem_limit_bytes=VMEM_LIMIT),
    )(x, comb, lw['weg'], lw['weu'], lw['wed'], lw['l2g'], lw['l2b'])


def _rope_tables(S):
    inv_freq = ROPE_THETA ** (-jnp.arange(0, QK_ROPE, 2, dtype=F32) / QK_ROPE)
    ang = jnp.arange(S, dtype=F32)[:, None] * inv_freq[None, :]
    cos, sin = jnp.cos(ang), jnp.sin(ang)
    z = lambda n: jnp.zeros((S, n), F32)
    tail = HEAD_PAD - QK_NOPE - QK_ROPE
    c = jnp.concatenate([jnp.ones((S, QK_NOPE), F32), cos, cos, z(tail)], axis=1)
    s_fwd = jnp.concatenate([z(QK_NOPE + HALF_ROPE), sin, z(tail)], axis=1)
    s_bwd = jnp.concatenate([z(QK_NOPE), -sin, z(HALF_ROPE + tail)], axis=1)
    qs = SOFTMAX_SCALE * math.log2(math.e)
    return (c * qs, s_fwd * qs, s_bwd * qs, c, s_fwd, s_bwd)


def _layer_weights(l, p):
    w_in, b_in = p['w_in'][l], p['b_in'][l]
    rope_lo = GLU_COLS + Q_LORA + KV_LORA
    gate_lo = rope_lo + QK_ROPE
    tail = HEAD_PAD - QK_NOPE - QK_ROPE
    pad_rope = lambda a: jnp.pad(a, ((0, 0), (QK_NOPE, tail)))
    w1 = jnp.concatenate([w_in[:, :rope_lo], pad_rope(w_in[:, rope_lo:gate_lo])], axis=1)
    b1 = jnp.concatenate([b_in[None, :rope_lo], pad_rope(b_in[None, rope_lo:gate_lo])], axis=1)
    wq = jnp.pad(p['w_q_b'][l].reshape(Q_LORA, N_HEADS, QK_NOPE + QK_ROPE), ((0, 0), (0, 0), (0, tail)))
    wkv = p['w_kv_b'][l].reshape(KV_LORA, N_HEADS, QK_NOPE + V_HEAD)
    wk = jnp.pad(wkv[:, :, :QK_NOPE], ((0, 0), (0, 0), (0, HEAD_PAD - QK_NOPE)))
    v_cols = wkv[:, :, QK_NOPE:]
    v_even = jnp.pad(v_cols, ((0, 0), (0, 0), (0, HEAD_PAD - V_HEAD)))
    v_odd = jnp.pad(v_cols, ((0, 0), (0, 0), (HEAD_PAD - V_HEAD, 0)))
    is_even = (jnp.arange(N_HEADS) % 2 == 0)[None, :, None]
    wv = jnp.where(is_even, v_even, v_odd)
    one_lane = jnp.where(jnp.arange(N_HEADS) % 2 == 0, V_HEAD, 0)
    vone = (jnp.arange(HEAD_PAD)[None, :] == one_lane[:, None]).astype(F32).reshape(1, N_HEADS * HEAD_PAD)
    w_r = jnp.pad(jnp.concatenate([p['w_router_expert'][l], p['w_router_group'][l]], axis=1),
                  ((0, 0), (0, LANES - N_EXPERTS - N_GROUPS)))
    b_r = jnp.pad(jnp.concatenate([p['b_router_expert'][l], p['b_router_group'][l]])[None, :],
                  ((0, 0), (0, LANES - N_EXPERTS - N_GROUPS)))
    w_r_hi = w_r.astype(BF16)
    w_r_lo = (w_r - w_r_hi.astype(F32)).astype(BF16)
    r2 = lambda a: a[None, :]
    return dict(
        w1=w1.astype(BF16), b1=b1, qg=r2(p['q_norm_g'][l]), wq=wq.reshape(Q_LORA, -1).astype(BF16),
        kvg=r2(p['kv_norm_g'][l]), wk=wk.reshape(KV_LORA, -1).astype(BF16),
        wv=wv.reshape(KV_LORA, -1).astype(BF16), vone=vone,
        dwk=p['dw_kernel'][l], dwb=r2(p['dw_bias'][l]), clg=r2(p['conv_ln_g'][l]), clb=r2(p['conv_ln_b'][l]),
        wco=p['w_conv_out'][l].astype(BF16), wg=w_in[:, gate_lo:].astype(BF16), bg=b_in[None, gate_lo:],
        wmo=p['w_mla_out'][l].astype(BF16), wo=p['w_out'][l].astype(BF16),
        l1g=r2(p['ln1_g'][l]), l1b=r2(p['ln1_b'][l]), wrh=w_r_hi, wrl=w_r_lo, br=b_r,
        weg=p['w_gate_e'][l].astype(BF16), weu=p['w_up_e'][l].astype(BF16), wed=p['w_down_e'][l].astype(BF16),
        l2g=r2(p['ln2_g'][l]), l2b=r2(p['ln2_b'][l]),
    )


def _trunk(x, S, p, *, tm_proj=512, tq=512, tk=512, tm_post=256, tm_moe=1024):
    tabs = _rope_tables(S)
    x = _layer_norm_call(x, p['ln_in_g'][None, :], p['ln_in_b'][None, :], tm_proj)
    for l in range(DEPTH):
        lw = _layer_weights(l, p)
        u, q, k, v = _proj_call(x, lw, tabs, S, tm_proj)
        o = _attn_call(q, k, v, S, tq, tk)
        x1, comb = _post_call(x, u, o, lw, S, tm_post)
        x = _moe_call(x1, comb, lw, tm_moe)
    return x


def kernel(x_prompt, x_sample, ln_in_g, ln_in_b, w_in, b_in, dw_kernel, dw_bias, conv_ln_g, conv_ln_b, w_conv_out, q_norm_g, w_q_b, kv_norm_g, w_kv_b, w_mla_out, w_out, ln1_g, ln1_b, w_router_group, b_router_group, w_router_expert, b_router_expert, w_gate_e, w_up_e, w_down_e, ln2_g, ln2_b):
    p = dict(ln_in_g=ln_in_g, ln_in_b=ln_in_b, w_in=w_in, b_in=b_in, dw_kernel=dw_kernel, dw_bias=dw_bias,
             conv_ln_g=conv_ln_g, conv_ln_b=conv_ln_b, w_conv_out=w_conv_out, q_norm_g=q_norm_g, w_q_b=w_q_b,
             kv_norm_g=kv_norm_g, w_kv_b=w_kv_b, w_mla_out=w_mla_out, w_out=w_out, ln1_g=ln1_g, ln1_b=ln1_b,
             w_router_group=w_router_group, b_router_group=b_router_group,
             w_router_expert=w_router_expert, b_router_expert=b_router_expert,
             w_gate_e=w_gate_e, w_up_e=w_up_e, w_down_e=w_down_e, ln2_g=ln2_g, ln2_b=ln2_b)
    S = x_prompt.shape[1]
    assert x_sample.shape[1] == S and x_prompt.shape[2] == D_MODEL
    n_prompt = x_prompt.shape[0] * S
    x = jnp.concatenate([x_prompt.reshape(-1, D_MODEL), x_sample.reshape(-1, D_MODEL)], axis=0)
    y = _trunk(x, S, p)
    return (y[:n_prompt].reshape(x_prompt.shape), y[n_prompt:].reshape(x_sample.shape))
```

```python
import functools
import math

import jax
import jax.numpy as jnp
from jax import lax
from jax.experimental import pallas as pl
from jax.experimental.pallas import tpu as pltpu

D_MODEL = 1024
DEPTH = 2
CONV_DIM = 512
CONV_WIDTH = 31
N_HEADS = 8
QK_NOPE = 64
QK_ROPE = 32
V_HEAD = 64
Q_LORA = 384
KV_LORA = 256
ROPE_THETA = 10000.0
SOFTMAX_SCALE = (QK_NOPE + QK_ROPE) ** -0.5
N_GROUPS = 4
EXPERTS_PER_GROUP = 8
N_EXPERTS = N_GROUPS * EXPERTS_PER_GROUP
D_EXPERT = 256
DEEPNORM_ALPHA = (2 * DEPTH) ** 0.25
LN_EPS = 1e-5

LANES = 128
HEAD_PAD = LANES
HALF_ROPE = QK_ROPE // 2
V_PAD = 80
HALO = 16
GLU_COLS = 2 * CONV_DIM
PROJ_COLS = GLU_COLS + Q_LORA + KV_LORA + HEAD_PAD
VMEM_LIMIT = 56 * 1024 * 1024

BF16 = jnp.bfloat16
F32 = jnp.float32


def _ln(x, g, b):
    mu = jnp.mean(x, axis=-1, keepdims=True)
    xc = x - mu
    var = jnp.mean(xc * xc, axis=-1, keepdims=True)
    return xc * lax.rsqrt(var + LN_EPS) * g + b


def _rms(x, g):
    return x * lax.rsqrt(jnp.mean(x * x, axis=-1, keepdims=True) + LN_EPS) * g


def _dot(a, b):
    return jnp.dot(a, b, preferred_element_type=F32)


def _rope(x, c, s_fwd, s_bwd):
    return x * c + pltpu.roll(x, HALF_ROPE, 1) * s_fwd + pltpu.roll(x, HEAD_PAD - HALF_ROPE, 1) * s_bwd


def _rope_t(x, c, s_fwd, s_bwd):
    return x * c + pltpu.roll(x, HALF_ROPE, 0) * s_fwd + pltpu.roll(x, HEAD_PAD - HALF_ROPE, 0) * s_bwd


def _dot_nt(a, b):
    return lax.dot_general(a, b, (((1,), (1,)), ((), ())), preferred_element_type=F32)


def _ln_kernel(x_ref, g_ref, b_ref, o_ref):
    o_ref[...] = _ln(x_ref[...], g_ref[...], b_ref[...])


def _layer_norm_call(x, g, b, tm):
    T = x.shape[0]
    row = lambda i: (i, 0)
    const = lambda i: (0, 0)
    return pl.pallas_call(
        _ln_kernel,
        grid=(T // tm,),
        in_specs=[pl.BlockSpec((tm, D_MODEL), row), pl.BlockSpec((1, D_MODEL), const),
                  pl.BlockSpec((1, D_MODEL), const)],
        out_specs=pl.BlockSpec((tm, D_MODEL), row),
        out_shape=jax.ShapeDtypeStruct((T, D_MODEL), F32),
        compiler_params=pltpu.CompilerParams(dimension_semantics=("parallel",), vmem_limit_bytes=VMEM_LIMIT),
    )(x, g, b)


def _proj_kernel(x_ref, w1_ref, b1_ref, qg_ref, wqt_ref, kvg_ref, wk_ref, wvt_ref, vone_ref,
                 cq_ref, sfq_ref, sbq_ref, ck_ref, sfk_ref, sbk_ref,
                 u_ref, qt_ref, k_ref, vt_ref):
    x = x_ref[...].astype(BF16)
    h = _dot(x, w1_ref[...]) + b1_ref[...]
    a = h[:, :CONV_DIM]
    g = h[:, CONV_DIM:GLU_COLS]
    u_ref[...] = a * jax.nn.sigmoid(g)
    qa = h[:, GLU_COLS:GLU_COLS + Q_LORA]
    kva = h[:, GLU_COLS + Q_LORA:GLU_COLS + Q_LORA + KV_LORA]
    kr = h[:, GLU_COLS + Q_LORA + KV_LORA:]

    qt = _dot_nt(wqt_ref[...], _rms(qa, qg_ref[...]).astype(BF16))
    cq, sfq, sbq = cq_ref[...], sfq_ref[...], sbq_ref[...]
    for hd in range(N_HEADS):
        sl = slice(hd * HEAD_PAD, (hd + 1) * HEAD_PAD)
        qt_ref[sl, :] = _rope_t(qt[sl, :], cq, sfq, sbq).astype(BF16)

    kvn = _rms(kva, kvg_ref[...]).astype(BF16)
    kn = _dot(kvn, wk_ref[...])
    krr = _rope(kr, ck_ref[...], sfk_ref[...], sbk_ref[...])
    for hd in range(N_HEADS):
        sl = slice(hd * HEAD_PAD, (hd + 1) * HEAD_PAD)
        k_ref[:, sl] = (kn[:, sl] + krr).astype(BF16)
    vt_ref[...] = (_dot_nt(wvt_ref[...], kvn) + vone_ref[...]).astype(BF16)


def _proj_call(x, lw, tabs, S, tm):
    T = x.shape[0]
    nper = S // tm
    row = lambda i: (i, 0)
    col = lambda i: (0, i)
    const = lambda i: (0, 0)
    full = lambda a: pl.BlockSpec(a.shape, const)
    tab_spec = pl.BlockSpec((tm, HEAD_PAD), lambda i: (i % nper, 0))
    tabt_spec = pl.BlockSpec((HEAD_PAD, tm), lambda i: (0, i % nper))
    wide = N_HEADS * HEAD_PAD
    vrows = N_HEADS * V_PAD
    return pl.pallas_call(
        _proj_kernel,
        grid=(T // tm,),
        in_specs=[pl.BlockSpec((tm, D_MODEL), row), full(lw['w1']), full(lw['b1']), full(lw['qg']),
                  full(lw['wqt']), full(lw['kvg']), full(lw['wk']), full(lw['wvt']), full(lw['vone'])]
                 + [tabt_spec] * 3 + [tab_spec] * 3,
        out_specs=[pl.BlockSpec((tm, CONV_DIM), row), pl.BlockSpec((wide, tm), col),
                   pl.BlockSpec((tm, wide), row), pl.BlockSpec((vrows, tm), col)],
        out_shape=[jax.ShapeDtypeStruct((T, CONV_DIM), F32), jax.ShapeDtypeStruct((wide, T), BF16),
                   jax.ShapeDtypeStruct((T, wide), BF16), jax.ShapeDtypeStruct((vrows, T), BF16)],
        compiler_params=pltpu.CompilerParams(dimension_semantics=("parallel",), vmem_limit_bytes=VMEM_LIMIT),
    )(x, lw['w1'], lw['b1'], lw['qg'], lw['wqt'], lw['kvg'], lw['wk'], lw['wvt'], lw['vone'], *tabs)


def _attn_kernel(qt_ref, k_ref, vt_ref, o_ref, s_scr, m_scr, acc_scr, *, tk):
    n = k_ref.shape[0] // tk

    def key_rows(j):
        return pl.ds(pl.multiple_of(j * tk, tk), tk)

    def scores(j, slot):
        for hh in range(2):
            sl = slice(hh * HEAD_PAD, (hh + 1) * HEAD_PAD)
            s_scr[slot, hh] = _dot(k_ref[key_rows(j), sl], qt_ref[sl, :])

    def update(j, slot):
        for hh in range(2):
            s = s_scr[slot, hh]
            m_old = m_scr[hh]
            m_new = jnp.maximum(m_old, jnp.max(s, axis=0, keepdims=True))
            p = jnp.exp2(s - m_new).astype(BF16)
            vt = vt_ref[hh * V_PAD:(hh + 1) * V_PAD, key_rows(j)]
            acc_scr[hh] = jnp.exp2(m_old - m_new) * acc_scr[hh] + _dot(vt, p)
            m_scr[hh] = m_new

    m_scr[...] = jnp.full(m_scr.shape, -jnp.inf, F32)
    acc_scr[...] = jnp.zeros(acc_scr.shape, F32)
    scores(0, 0)

    def body(jj, carry):
        scores(2 * jj + 1, 1)
        update(2 * jj, 0)
        scores(2 * jj + 2, 0)
        update(2 * jj + 1, 1)
        return carry

    lax.fori_loop(0, n // 2 - 1, body, 0)
    scores(n - 1, 1)
    update(n - 2, 0)
    update(n - 1, 1)
    ot = jnp.concatenate([acc_scr[hh, :V_HEAD, :] / acc_scr[hh, V_HEAD:V_HEAD + 1, :] for hh in range(2)], axis=0)
    o_ref[...] = ot.T.astype(BF16)


def _attn_call(qt, k, vt, S, tq, tk):
    T = k.shape[0]
    nb = T // S
    nq = S // tq
    assert (S // tk) % 2 == 0
    return pl.pallas_call(
        functools.partial(_attn_kernel, tk=tk),
        grid=(nb, N_HEADS // 2, nq),
        in_specs=[pl.BlockSpec((2 * HEAD_PAD, tq), lambda b, hp, i: (hp, b * nq + i)),
                  pl.BlockSpec((S, 2 * HEAD_PAD), lambda b, hp, i: (b, hp)),
                  pl.BlockSpec((2 * V_PAD, S), lambda b, hp, i: (hp, b))],
        out_specs=pl.BlockSpec((tq, 2 * V_HEAD), lambda b, hp, i: (b * nq + i, hp)),
        out_shape=jax.ShapeDtypeStruct((T, N_HEADS * V_HEAD), BF16),
        scratch_shapes=[pltpu.VMEM((2, 2, tk, tq), F32), pltpu.VMEM((2, 1, tq), F32),
                        pltpu.VMEM((2, V_PAD, tq), F32)],
        compiler_params=pltpu.CompilerParams(dimension_semantics=("parallel", "parallel", "parallel"),
                                             vmem_limit_bytes=VMEM_LIMIT),
    )(qt, k, vt)


def _route(logits):
    lane = lax.broadcasted_iota(jnp.int32, logits.shape, 1)
    neg = -jnp.inf
    is_group = (lane >= N_EXPERTS) & (lane < N_EXPERTS + N_GROUPS)
    gl = jnp.where(is_group, logits, neg)
    gmax = jnp.max(gl, axis=1, keepdims=True)
    g_w = 1.0 / jnp.sum(jnp.exp(gl - gmax), axis=1, keepdims=True)
    g_idx = jnp.min(jnp.where(gl == gmax, lane, 2 * LANES), axis=1, keepdims=True) - N_EXPERTS
    lo = g_idx * EXPERTS_PER_GROUP
    el = jnp.where((lane >= lo) & (lane < lo + EXPERTS_PER_GROUP), logits, neg)
    m1 = jnp.max(el, axis=1, keepdims=True)
    i1 = jnp.min(jnp.where(el == m1, lane, 2 * LANES), axis=1, keepdims=True)
    el2 = jnp.where(lane == i1, neg, el)
    m2 = jnp.max(el2, axis=1, keepdims=True)
    i2 = jnp.min(jnp.where(el2 == m2, lane, 2 * LANES), axis=1, keepdims=True)
    r = jnp.exp(m2 - m1)
    w1 = g_w / (1.0 + r)
    w2 = g_w * r / (1.0 + r)
    return i1, i2, w1, w2, lane


def _post_kernel(x_ref, u_ref, up_ref, un_ref, o_ref,
                 dwk_ref, dwb_ref, clg_ref, clb_ref, wco_ref, wg_ref, bg_ref, wmo_ref, wo_ref,
                 l1g_ref, l1b_ref, wrh_ref, wrl_ref, br_ref,
                 x1_ref, comb_ref, ubuf, *, nper):
    tm = x_ref.shape[0]
    i = pl.program_id(0)
    first = (i % nper) == 0
    last = (i % nper) == nper - 1
    ubuf[0:HALO, :] = jnp.where(first, 0.0, up_ref[...])
    ubuf[HALO:HALO + tm, :] = u_ref[...]
    ubuf[HALO + tm:, :] = jnp.where(last, 0.0, un_ref[...])
    conv = jnp.zeros((tm, CONV_DIM), F32) + dwb_ref[...]
    off = HALO - CONV_WIDTH // 2
    for w in range(CONV_WIDTH):
        conv = conv + ubuf[off + w:off + w + tm, :] * dwk_ref[w:w + 1, :]
    c = _ln(conv, clg_ref[...], clb_ref[...])
    c = c * jax.nn.sigmoid(c)
    conv_out = _dot(c.astype(BF16), wco_ref[...])

    x = x_ref[...]
    gates = jax.nn.sigmoid(_dot(x.astype(BF16), wg_ref[...]) + bg_ref[...])
    mla_out = _dot(o_ref[...], wmo_ref[...])
    merged = gates[:, :D_MODEL] * conv_out + gates[:, D_MODEL:] * mla_out
    m = _dot(merged.astype(BF16), wo_ref[...])
    x1 = _ln(DEEPNORM_ALPHA * x + m, l1g_ref[...], l1b_ref[...])
    x1_ref[...] = x1

    hi = x1.astype(BF16)
    lo = (x1 - hi.astype(F32)).astype(BF16)
    logits = _dot(hi, wrh_ref[...]) + _dot(lo, wrh_ref[...]) + _dot(hi, wrl_ref[...]) + br_ref[...]
    i1, i2, w1, w2, lane = _route(logits)
    comb_ref[...] = jnp.where(lane == i1, w1, 0.0) + jnp.where(lane == i2, w2, 0.0)


def _post_call(x, u, o, lw, S, tm):
    T = x.shape[0]
    nper = S // tm
    nh = tm // HALO
    last_halo = T // HALO - 1
    row = lambda i: (i, 0)
    const = lambda i: (0, 0)
    full = lambda a: pl.BlockSpec(a.shape, const)
    names = ['dwk', 'dwb', 'clg', 'clb', 'wco', 'wg', 'bg', 'wmo', 'wo', 'l1g', 'l1b', 'wrh', 'wrl', 'br']
    ws = [lw[n] for n in names]
    return pl.pallas_call(
        functools.partial(_post_kernel, nper=nper),
        grid=(T // tm,),
        in_specs=[pl.BlockSpec((tm, D_MODEL), row), pl.BlockSpec((tm, CONV_DIM), row),
                  pl.BlockSpec((HALO, CONV_DIM), lambda i: (jnp.maximum(i * nh - 1, 0), 0)),
                  pl.BlockSpec((HALO, CONV_DIM), lambda i: (jnp.minimum((i + 1) * nh, last_halo), 0)),
                  pl.BlockSpec((tm, N_HEADS * V_HEAD), row)] + [full(a) for a in ws],
        out_specs=[pl.BlockSpec((tm, D_MODEL), row), pl.BlockSpec((tm, LANES), row)],
        out_shape=[jax.ShapeDtypeStruct((T, D_MODEL), F32), jax.ShapeDtypeStruct((T, LANES), F32)],
        scratch_shapes=[pltpu.VMEM((tm + 2 * HALO, CONV_DIM), F32)],
        compiler_params=pltpu.CompilerParams(dimension_semantics=("parallel",), vmem_limit_bytes=VMEM_LIMIT),
    )(x, u, u, u, o, *ws)


def _moe_kernel(x_ref, comb_ref, wg_ref, wu_ref, wd_ref, l2g_ref, l2b_ref, o_ref, acc_ref):
    e = pl.program_id(1)

    @pl.when(e == 0)
    def _():
        acc_ref[...] = jnp.zeros_like(acc_ref)

    xb = x_ref[...].astype(BF16)
    gate = _dot(xb, wg_ref[0])
    hdn = gate * jax.nn.sigmoid(gate) * _dot(xb, wu_ref[0])
    y = _dot(hdn.astype(BF16), wd_ref[0])
    comb = comb_ref[...]
    lane = lax.broadcasted_iota(jnp.int32, comb.shape, 1)
    c = jnp.sum(jnp.where(lane == e, comb, 0.0), axis=1, keepdims=True)
    acc_ref[...] += c * y

    @pl.when(e == N_EXPERTS - 1)
    def _():
        o_ref[...] = _ln(DEEPNORM_ALPHA * x_ref[...] + acc_ref[...], l2g_ref[...], l2b_ref[...])


def _moe_call(x, comb, lw, tm):
    T = x.shape[0]
    row = lambda i, e: (i, 0)
    const = lambda i, e: (0, 0)
    ex = lambda i, e: (e, 0, 0)
    return pl.pallas_call(
        _moe_kernel,
        grid=(T // tm, N_EXPERTS),
        in_specs=[pl.BlockSpec((tm, D_MODEL), row), pl.BlockSpec((tm, LANES), row),
                  pl.BlockSpec((1, D_MODEL, D_EXPERT), ex), pl.BlockSpec((1, D_MODEL, D_EXPERT), ex),
                  pl.BlockSpec((1, D_EXPERT, D_MODEL), ex),
                  pl.BlockSpec((1, D_MODEL), const), pl.BlockSpec((1, D_MODEL), const)],
        out_specs=pl.BlockSpec((tm, D_MODEL), row),
        out_shape=jax.ShapeDtypeStruct((T, D_MODEL), F32),
        scratch_shapes=[pltpu.VMEM((tm, D_MODEL), F32)],
        compiler_params=pltpu.CompilerParams(dimension_semantics=("parallel", "arbitrary"),
                                             vmem_limit_bytes=VMEM_LIMIT),
    )(x, comb, lw['weg'], lw['weu'], lw['wed'], lw['l2g'], lw['l2b'])


def _rope_tables(S):
    inv_freq = ROPE_THETA ** (-jnp.arange(0, QK_ROPE, 2, dtype=F32) / QK_ROPE)
    ang = jnp.arange(S, dtype=F32)[:, None] * inv_freq[None, :]
    cos, sin = jnp.cos(ang), jnp.sin(ang)
    z = lambda n: jnp.zeros((S, n), F32)
    tail = HEAD_PAD - QK_NOPE - QK_ROPE
    c = jnp.concatenate([jnp.ones((S, QK_NOPE), F32), cos, cos, z(tail)], axis=1)
    s_fwd = jnp.concatenate([z(QK_NOPE + HALF_ROPE), sin, z(tail)], axis=1)
    s_bwd = jnp.concatenate([z(QK_NOPE), -sin, z(HALF_ROPE + tail)], axis=1)
    qs = SOFTMAX_SCALE * math.log2(math.e)
    return ((c * qs).T, (s_fwd * qs).T, (s_bwd * qs).T, c, s_fwd, s_bwd)


def _layer_weights(l, p):
    w_in, b_in = p['w_in'][l], p['b_in'][l]
    rope_lo = GLU_COLS + Q_LORA + KV_LORA
    gate_lo = rope_lo + QK_ROPE
    tail = HEAD_PAD - QK_NOPE - QK_ROPE
    pad_rope = lambda a: jnp.pad(a, ((0, 0), (QK_NOPE, tail)))
    w1 = jnp.concatenate([w_in[:, :rope_lo], pad_rope(w_in[:, rope_lo:gate_lo])], axis=1)
    b1 = jnp.concatenate([b_in[None, :rope_lo], pad_rope(b_in[None, rope_lo:gate_lo])], axis=1)
    wq = jnp.pad(p['w_q_b'][l].reshape(Q_LORA, N_HEADS, QK_NOPE + QK_ROPE), ((0, 0), (0, 0), (0, tail)))
    wkv = p['w_kv_b'][l].reshape(KV_LORA, N_HEADS, QK_NOPE + V_HEAD)
    wk = jnp.pad(wkv[:, :, :QK_NOPE], ((0, 0), (0, 0), (0, HEAD_PAD - QK_NOPE)))
    wv = jnp.pad(wkv[:, :, QK_NOPE:], ((0, 0), (0, 0), (0, V_PAD - V_HEAD)))
    vone = jnp.tile(jnp.arange(V_PAD) == V_HEAD, N_HEADS).astype(F32)[:, None]
    w_r = jnp.pad(jnp.concatenate([p['w_router_expert'][l], p['w_router_group'][l]], axis=1),
                  ((0, 0), (0, LANES - N_EXPERTS - N_GROUPS)))
    b_r = jnp.pad(jnp.concatenate([p['b_router_expert'][l], p['b_router_group'][l]])[None, :],
                  ((0, 0), (0, LANES - N_EXPERTS - N_GROUPS)))
    w_r_hi = w_r.astype(BF16)
    w_r_lo = (w_r - w_r_hi.astype(F32)).astype(BF16)
    r2 = lambda a: a[None, :]
    return dict(
        w1=w1.astype(BF16), b1=b1, qg=r2(p['q_norm_g'][l]), wqt=wq.reshape(Q_LORA, -1).T.astype(BF16),
        kvg=r2(p['kv_norm_g'][l]), wk=wk.reshape(KV_LORA, -1).astype(BF16),
        wvt=wv.reshape(KV_LORA, -1).T.astype(BF16), vone=vone,
        dwk=p['dw_kernel'][l], dwb=r2(p['dw_bias'][l]), clg=r2(p['conv_ln_g'][l]), clb=r2(p['conv_ln_b'][l]),
        wco=p['w_conv_out'][l].astype(BF16), wg=w_in[:, gate_lo:].astype(BF16), bg=b_in[None, gate_lo:],
        wmo=p['w_mla_out'][l].astype(BF16), wo=p['w_out'][l].astype(BF16),
        l1g=r2(p['ln1_g'][l]), l1b=r2(p['ln1_b'][l]), wrh=w_r_hi, wrl=w_r_lo, br=b_r,
        weg=p['w_gate_e'][l].astype(BF16), weu=p['w_up_e'][l].astype(BF16), wed=p['w_down_e'][l].astype(BF16),
        l2g=r2(p['ln2_g'][l]), l2b=r2(p['ln2_b'][l]),
    )


def _trunk(x, S, p, *, tm_proj=512, tq=512, tk=1024, tm_post=256, tm_moe=1024):
    tabs = _rope_tables(S)
    x = _layer_norm_call(x, p['ln_in_g'][None, :], p['ln_in_b'][None, :], tm_proj)
    for l in range(DEPTH):
        lw = _layer_weights(l, p)
        u, qt, k, vt = _proj_call(x, lw, tabs, S, tm_proj)
        o = _attn_call(qt, k, vt, S, tq, tk)
        x1, comb = _post_call(x, u, o, lw, S, tm_post)
        x = _moe_call(x1, comb, lw, tm_moe)
    return x


def kernel(x_prompt, x_sample, ln_in_g, ln_in_b, w_in, b_in, dw_kernel, dw_bias, conv_ln_g, conv_ln_b, w_conv_out, q_norm_g, w_q_b, kv_norm_g, w_kv_b, w_mla_out, w_out, ln1_g, ln1_b, w_router_group, b_router_group, w_router_expert, b_router_expert, w_gate_e, w_up_e, w_down_e, ln2_g, ln2_b):
    p = dict(ln_in_g=ln_in_g, ln_in_b=ln_in_b, w_in=w_in, b_in=b_in, dw_kernel=dw_kernel, dw_bias=dw_bias,
             conv_ln_g=conv_ln_g, conv_ln_b=conv_ln_b, w_conv_out=w_conv_out, q_norm_g=q_norm_g, w_q_b=w_q_b,
             kv_norm_g=kv_norm_g, w_kv_b=w_kv_b, w_mla_out=w_mla_out, w_out=w_out, ln1_g=ln1_g, ln1_b=ln1_b,
             w_router_group=w_router_group, b_router_group=b_router_group,
             w_router_expert=w_router_expert, b_router_expert=b_router_expert,
             w_gate_e=w_gate_e, w_up_e=w_up_e, w_down_e=w_down_e, ln2_g=ln2_g, ln2_b=ln2_b)
    S = x_prompt.shape[1]
    assert x_sample.shape[1] == S and x_prompt.shape[2] == D_MODEL
    n_prompt = x_prompt.shape[0] * S
    x = jnp.concatenate([x_prompt.reshape(-1, D_MODEL), x_sample.reshape(-1, D_MODEL)], axis=0)
    y = _trunk(x, S, p)
    return (y[:n_prompt].reshape(x_prompt.shape), y[n_prompt:].reshape(x_sample.shape))
```

```python
import functools
import math

import jax
import jax.numpy as jnp
from jax import lax
from jax.experimental import pallas as pl
from jax.experimental.pallas import tpu as pltpu

D_MODEL = 1024
DEPTH = 2
CONV_DIM = 512
CONV_WIDTH = 31
N_HEADS = 8
QK_NOPE = 64
QK_ROPE = 32
V_HEAD = 64
Q_LORA = 384
KV_LORA = 256
ROPE_THETA = 10000.0
SOFTMAX_SCALE = (QK_NOPE + QK_ROPE) ** -0.5
N_GROUPS = 4
EXPERTS_PER_GROUP = 8
N_EXPERTS = N_GROUPS * EXPERTS_PER_GROUP
D_EXPERT = 256
DEEPNORM_ALPHA = (2 * DEPTH) ** 0.25
LN_EPS = 1e-5

LANES = 128
HEAD_PAD = LANES
HALF_ROPE = QK_ROPE // 2
V_PAD = 80
HALO = 16
GLU_COLS = 2 * CONV_DIM
SUBLANES = 8
PAIRS_PER_GROUP = EXPERTS_PER_GROUP * (EXPERTS_PER_GROUP - 1) // 2
N_CLASSES = N_GROUPS * PAIRS_PER_GROUP
VMEM_LIMIT = 56 * 1024 * 1024

BF16 = jnp.bfloat16
F32 = jnp.float32


def _ln(x, g, b):
    mu = jnp.mean(x, axis=-1, keepdims=True)
    xc = x - mu
    var = jnp.mean(xc * xc, axis=-1, keepdims=True)
    return xc * lax.rsqrt(var + LN_EPS) * g + b


def _rms(x, g):
    return x * lax.rsqrt(jnp.mean(x * x, axis=-1, keepdims=True) + LN_EPS) * g


def _dot(a, b):
    return jnp.dot(a, b, preferred_element_type=F32)


def _rope(x, c, s_fwd, s_bwd):
    return x * c + pltpu.roll(x, HALF_ROPE, 1) * s_fwd + pltpu.roll(x, HEAD_PAD - HALF_ROPE, 1) * s_bwd


def _rope_t(x, c, s_fwd, s_bwd):
    return x * c + pltpu.roll(x, HALF_ROPE, 0) * s_fwd + pltpu.roll(x, HEAD_PAD - HALF_ROPE, 0) * s_bwd


def _dot_nt(a, b):
    return lax.dot_general(a, b, (((1,), (1,)), ((), ())), preferred_element_type=F32)


def _ln_kernel(x_ref, g_ref, b_ref, o_ref):
    o_ref[...] = _ln(x_ref[...], g_ref[...], b_ref[...])


def _layer_norm_call(x, g, b, tm):
    T = x.shape[0]
    row = lambda i: (i, 0)
    const = lambda i: (0, 0)
    return pl.pallas_call(
        _ln_kernel,
        grid=(T // tm,),
        in_specs=[pl.BlockSpec((tm, D_MODEL), row), pl.BlockSpec((1, D_MODEL), const),
                  pl.BlockSpec((1, D_MODEL), const)],
        out_specs=pl.BlockSpec((tm, D_MODEL), row),
        out_shape=jax.ShapeDtypeStruct((T, D_MODEL), F32),
        compiler_params=pltpu.CompilerParams(dimension_semantics=("parallel",), vmem_limit_bytes=VMEM_LIMIT),
    )(x, g, b)


def _proj_kernel(x_ref, w1_ref, b1_ref, qg_ref, wqt_ref, kvg_ref, wk_ref, wvt_ref, vone_ref,
                 cq_ref, sfq_ref, sbq_ref, ck_ref, sfk_ref, sbk_ref,
                 u_ref, qt_ref, k_ref, vt_ref):
    x = x_ref[...].astype(BF16)
    h = _dot(x, w1_ref[...]) + b1_ref[...]
    a = h[:, :CONV_DIM]
    g = h[:, CONV_DIM:GLU_COLS]
    u_ref[...] = a * jax.nn.sigmoid(g)
    qa = h[:, GLU_COLS:GLU_COLS + Q_LORA]
    kva = h[:, GLU_COLS + Q_LORA:GLU_COLS + Q_LORA + KV_LORA]
    kr = h[:, GLU_COLS + Q_LORA + KV_LORA:]

    qt = _dot_nt(wqt_ref[...], _rms(qa, qg_ref[...]).astype(BF16))
    cq, sfq, sbq = cq_ref[...], sfq_ref[...], sbq_ref[...]
    for hd in range(N_HEADS):
        sl = slice(hd * HEAD_PAD, (hd + 1) * HEAD_PAD)
        qt_ref[sl, :] = _rope_t(qt[sl, :], cq, sfq, sbq).astype(BF16)

    kvn = _rms(kva, kvg_ref[...]).astype(BF16)
    kn = _dot(kvn, wk_ref[...])
    krr = _rope(kr, ck_ref[...], sfk_ref[...], sbk_ref[...])
    for hd in range(N_HEADS):
        sl = slice(hd * HEAD_PAD, (hd + 1) * HEAD_PAD)
        k_ref[:, sl] = (kn[:, sl] + krr).astype(BF16)
    vt_ref[...] = (_dot_nt(wvt_ref[...], kvn) + vone_ref[...]).astype(BF16)


def _proj_call(x, lw, tabs, S, tm):
    T = x.shape[0]
    nper = S // tm
    row = lambda i: (i, 0)
    col = lambda i: (0, i)
    const = lambda i: (0, 0)
    full = lambda a: pl.BlockSpec(a.shape, const)
    tab_spec = pl.BlockSpec((tm, HEAD_PAD), lambda i: (i % nper, 0))
    tabt_spec = pl.BlockSpec((HEAD_PAD, tm), lambda i: (0, i % nper))
    wide = N_HEADS * HEAD_PAD
    vrows = N_HEADS * V_PAD
    return pl.pallas_call(
        _proj_kernel,
        grid=(T // tm,),
        in_specs=[pl.BlockSpec((tm, D_MODEL), row), full(lw['w1']), full(lw['b1']), full(lw['qg']),
                  full(lw['wqt']), full(lw['kvg']), full(lw['wk']), full(lw['wvt']), full(lw['vone'])]
                 + [tabt_spec] * 3 + [tab_spec] * 3,
        out_specs=[pl.BlockSpec((tm, CONV_DIM), row), pl.BlockSpec((wide, tm), col),
                   pl.BlockSpec((tm, wide), row), pl.BlockSpec((vrows, tm), col)],
        out_shape=[jax.ShapeDtypeStruct((T, CONV_DIM), F32), jax.ShapeDtypeStruct((wide, T), BF16),
                   jax.ShapeDtypeStruct((T, wide), BF16), jax.ShapeDtypeStruct((vrows, T), BF16)],
        compiler_params=pltpu.CompilerParams(dimension_semantics=("parallel",), vmem_limit_bytes=VMEM_LIMIT),
    )(x, lw['w1'], lw['b1'], lw['qg'], lw['wqt'], lw['kvg'], lw['wk'], lw['wvt'], lw['vone'], *tabs)


def _attn_kernel(qt_ref, k_ref, vt_ref, o_ref, s_scr, m_scr, acc_scr, *, tk):
    n = k_ref.shape[0] // tk

    def key_rows(j):
        return pl.ds(pl.multiple_of(j * tk, tk), tk)

    def scores(j, slot):
        for hh in range(2):
            sl = slice(hh * HEAD_PAD, (hh + 1) * HEAD_PAD)
            s_scr[slot, hh] = _dot(k_ref[key_rows(j), sl], qt_ref[sl, :])

    def update(j, slot):
        for hh in range(2):
            s = s_scr[slot, hh]
            m_old = m_scr[hh]
            m_new = jnp.maximum(m_old, jnp.max(s, axis=0, keepdims=True))
            p = jnp.exp2(s - m_new).astype(BF16)
            vt = vt_ref[hh * V_PAD:(hh + 1) * V_PAD, key_rows(j)]
            acc_scr[hh] = jnp.exp2(m_old - m_new) * acc_scr[hh] + _dot(vt, p)
            m_scr[hh] = m_new

    m_scr[...] = jnp.full(m_scr.shape, -jnp.inf, F32)
    acc_scr[...] = jnp.zeros(acc_scr.shape, F32)
    scores(0, 0)

    def body(jj, carry):
        scores(2 * jj + 1, 1)
        update(2 * jj, 0)
        scores(2 * jj + 2, 0)
        update(2 * jj + 1, 1)
        return carry

    lax.fori_loop(0, n // 2 - 1, body, 0)
    scores(n - 1, 1)
    update(n - 2, 0)
    update(n - 1, 1)
    ot = jnp.concatenate([acc_scr[hh, :V_HEAD, :] / acc_scr[hh, V_HEAD:V_HEAD + 1, :] for hh in range(2)], axis=0)
    o_ref[...] = ot.T.astype(BF16)


def _attn_call(qt, k, vt, S, tq, tk):
    T = k.shape[0]
    nb = T // S
    nq = S // tq
    assert (S // tk) % 2 == 0
    return pl.pallas_call(
        functools.partial(_attn_kernel, tk=tk),
        grid=(nb, N_HEADS // 2, nq),
        in_specs=[pl.BlockSpec((2 * HEAD_PAD, tq), lambda b, hp, i: (hp, b * nq + i)),
                  pl.BlockSpec((S, 2 * HEAD_PAD), lambda b, hp, i: (b, hp)),
                  pl.BlockSpec((2 * V_PAD, S), lambda b, hp, i: (hp, b))],
        out_specs=pl.BlockSpec((tq, 2 * V_HEAD), lambda b, hp, i: (b * nq + i, hp)),
        out_shape=jax.ShapeDtypeStruct((T, N_HEADS * V_HEAD), BF16),
        scratch_shapes=[pltpu.VMEM((2, 2, tk, tq), F32), pltpu.VMEM((2, 1, tq), F32),
                        pltpu.VMEM((2, V_PAD, tq), F32)],
        compiler_params=pltpu.CompilerParams(dimension_semantics=("parallel", "parallel", "parallel"),
                                             vmem_limit_bytes=VMEM_LIMIT),
    )(qt, k, vt)


def _route(logits):
    lane = lax.broadcasted_iota(jnp.int32, logits.shape, 1)
    neg = -jnp.inf
    is_group = (lane >= N_EXPERTS) & (lane < N_EXPERTS + N_GROUPS)
    gl = jnp.where(is_group, logits, neg)
    gmax = jnp.max(gl, axis=1, keepdims=True)
    g_w = 1.0 / jnp.sum(jnp.exp(gl - gmax), axis=1, keepdims=True)
    g_idx = jnp.min(jnp.where(gl == gmax, lane, 2 * LANES), axis=1, keepdims=True) - N_EXPERTS
    lo = g_idx * EXPERTS_PER_GROUP
    el = jnp.where((lane >= lo) & (lane < lo + EXPERTS_PER_GROUP), logits, neg)
    m1 = jnp.max(el, axis=1, keepdims=True)
    i1 = jnp.min(jnp.where(el == m1, lane, 2 * LANES), axis=1, keepdims=True)
    el2 = jnp.where(lane == i1, neg, el)
    m2 = jnp.max(el2, axis=1, keepdims=True)
    i2 = jnp.min(jnp.where(el2 == m2, lane, 2 * LANES), axis=1, keepdims=True)
    r = jnp.exp(m2 - m1)
    w1 = g_w / (1.0 + r)
    w2 = g_w * r / (1.0 + r)
    return g_idx, i1, i2, w1, w2, lane


def _post_kernel(x_ref, u_ref, up_ref, un_ref, o_ref,
                 dwk_ref, dwb_ref, clg_ref, clb_ref, wco_ref, wg_ref, bg_ref, wmo_ref, wo_ref,
                 l1g_ref, l1b_ref, wrh_ref, wrl_ref, br_ref,
                 x1_ref, route_ref, cnt_ref, ubuf, ush, cnt_scr, *, nper):
    tm = x_ref.shape[0]
    i = pl.program_id(0)
    first = (i % nper) == 0
    last = (i % nper) == nper - 1
    ubuf[0:HALO, :] = jnp.where(first, 0.0, up_ref[...])
    ubuf[HALO:HALO + tm, :] = u_ref[...]
    ubuf[HALO + tm:, :] = jnp.where(last, 0.0, un_ref[...])
    conv = jnp.zeros((tm, CONV_DIM), F32) + dwb_ref[...]
    off = HALO - CONV_WIDTH // 2
    span = ush.shape[1]
    for b in range(SUBLANES):
        ush[b] = ubuf[off + b:off + b + span, :]
    for b in range(SUBLANES):
        for w in range(b, CONV_WIDTH, SUBLANES):
            conv = conv + ush[b, w - b:w - b + tm, :] * dwk_ref[w:w + 1, :]
    c = _ln(conv, clg_ref[...], clb_ref[...])
    c = c * jax.nn.sigmoid(c)
    conv_out = _dot(c.astype(BF16), wco_ref[...])

    x = x_ref[...]
    gates = jax.nn.sigmoid(_dot(x.astype(BF16), wg_ref[...]) + bg_ref[...])
    mla_out = _dot(o_ref[...], wmo_ref[...])
    merged = gates[:, :D_MODEL] * conv_out + gates[:, D_MODEL:] * mla_out
    m = _dot(merged.astype(BF16), wo_ref[...])
    x1 = _ln(DEEPNORM_ALPHA * x + m, l1g_ref[...], l1b_ref[...])
    x1_ref[...] = x1

    hi = x1.astype(BF16)
    lo = (x1 - hi.astype(F32)).astype(BF16)
    logits = _dot(hi, wrh_ref[...]) + _dot(lo, wrh_ref[...]) + _dot(hi, wrl_ref[...]) + br_ref[...]
    g_idx, i1, i2, _, _, lane = _route(logits)
    la = jnp.minimum(i1, i2) - g_idx * EXPERTS_PER_GROUP
    lb = jnp.maximum(i1, i2) - g_idx * EXPERTS_PER_GROUP
    pair = jnp.right_shift(la * (2 * EXPERTS_PER_GROUP - 1 - la), 1) + (lb - la - 1)
    cls = g_idx * PAIRS_PER_GROUP + pair
    onehot = lane == cls

    @pl.when(i == 0)
    def _():
        cnt_scr[...] = jnp.zeros_like(cnt_scr)

    r_id = lax.broadcasted_iota(jnp.int32, (tm, tm), 0)
    c_id = lax.broadcasted_iota(jnp.int32, (tm, tm), 1)
    earlier = _dot((c_id < r_id).astype(BF16), onehot.astype(BF16))
    rank = jnp.sum(jnp.where(onehot, earlier + cnt_scr[...], 0.0), axis=1, keepdims=True)
    cnt_scr[...] += jnp.sum(onehot.astype(F32), axis=0, keepdims=True)
    cnt_ref[...] = cnt_scr[...]
    route_ref[...] = jnp.where(lane == 0, cls.astype(F32), jnp.where(lane == 1, rank, 0.0))


def _post_call(x, u, o, lw, S, tm):
    T = x.shape[0]
    nper = S // tm
    nh = tm // HALO
    last_halo = T // HALO - 1
    row = lambda i: (i, 0)
    const = lambda i: (0, 0)
    full = lambda a: pl.BlockSpec(a.shape, const)
    names = ['dwk', 'dwb', 'clg', 'clb', 'wco', 'wg', 'bg', 'wmo', 'wo', 'l1g', 'l1b', 'wrh', 'wrl', 'br']
    ws = [lw[n] for n in names]
    return pl.pallas_call(
        functools.partial(_post_kernel, nper=nper),
        grid=(T // tm,),
        in_specs=[pl.BlockSpec((tm, D_MODEL), row), pl.BlockSpec((tm, CONV_DIM), row),
                  pl.BlockSpec((HALO, CONV_DIM), lambda i: (jnp.maximum(i * nh - 1, 0), 0)),
                  pl.BlockSpec((HALO, CONV_DIM), lambda i: (jnp.minimum((i + 1) * nh, last_halo), 0)),
                  pl.BlockSpec((tm, N_HEADS * V_HEAD), row)] + [full(a) for a in ws],
        out_specs=[pl.BlockSpec((tm, D_MODEL), row), pl.BlockSpec((tm, LANES), row),
                   pl.BlockSpec((1, LANES), const)],
        out_shape=[jax.ShapeDtypeStruct((T, D_MODEL), F32), jax.ShapeDtypeStruct((T, LANES), F32),
                   jax.ShapeDtypeStruct((1, LANES), F32)],
        scratch_shapes=[pltpu.VMEM((tm + 2 * HALO, CONV_DIM), F32),
                        pltpu.VMEM((SUBLANES, tm + (CONV_WIDTH - 1) // SUBLANES * SUBLANES, CONV_DIM), F32),
                        pltpu.VMEM((1, LANES), F32)],
        compiler_params=pltpu.CompilerParams(dimension_semantics=("arbitrary",), vmem_limit_bytes=VMEM_LIMIT),
    )(x, u, u, u, o, *ws)


def _class_tables(route, cnt, tile, n_tiles):
    cls = route[:, 0].astype(jnp.int32)
    rank = route[:, 1].astype(jnp.int32)
    counts = cnt[0, :N_CLASSES].astype(jnp.int32)
    tiles_of = (counts + tile - 1) // tile
    t_end = jnp.cumsum(tiles_of)
    t_start = t_end - tiles_of
    pos = t_start[cls] * tile + rank
    g = jnp.arange(n_tiles, dtype=jnp.int32)
    blk = jnp.minimum(g, t_end[-1] - 1)
    c = jnp.searchsorted(t_end, blk, side='right').astype(jnp.int32)
    valid = jnp.where(g < t_end[-1], jnp.clip(counts[c] - (blk - t_start[c]) * tile, 0, tile), 0)
    pair_a, pair_b = zip(*[(a, b) for a in range(EXPERTS_PER_GROUP) for b in range(a + 1, EXPERTS_PER_GROUP)])
    grp, pr = c // PAIRS_PER_GROUP, c % PAIRS_PER_GROUP
    ea = grp * EXPERTS_PER_GROUP + jnp.asarray(pair_a, jnp.int32)[pr]
    eb = grp * EXPERTS_PER_GROUP + jnp.asarray(pair_b, jnp.int32)[pr]
    return pos, blk, ea, eb, valid.astype(jnp.int32)


def _row_copy(src_ref, src_row, dst_ref, dst_row, sem):
    return pltpu.make_async_copy(src_ref.at[pl.ds(src_row, 1)], dst_ref.at[pl.ds(dst_row, 1)], sem)


def _scatter_kernel(pos_ref, x_ref, xs_ref, sem):
    tm = x_ref.shape[0]

    def start(r, carry):
        _row_copy(x_ref, r, xs_ref, pos_ref[0, 0, r], sem).start()
        return carry

    lax.fori_loop(0, tm, start, 0, unroll=8)
    pltpu.make_async_copy(x_ref, xs_ref.at[pl.ds(0, tm)], sem).wait()


def _gather_kernel(pos_ref, xs_ref, o_ref, sem):
    tm = o_ref.shape[0]

    def start(r, carry):
        _row_copy(xs_ref, pos_ref[0, 0, r], o_ref, r, sem).start()
        return carry

    lax.fori_loop(0, tm, start, 0, unroll=8)
    pltpu.make_async_copy(xs_ref.at[pl.ds(0, tm)], o_ref, sem).wait()


def _permute_call(kernel_fn, pos, x, n_out, tm, to_sorted):
    T = pos.shape[0]
    pos3 = pos.reshape(T // tm, 1, tm)
    pos_spec = pl.BlockSpec((1, 1, tm), lambda i: (i, 0, 0), memory_space=pltpu.SMEM)
    tile_spec = pl.BlockSpec((tm, D_MODEL), lambda i: (i, 0))
    any_spec = pl.BlockSpec(memory_space=pl.ANY)
    return pl.pallas_call(
        kernel_fn,
        grid=(T // tm,),
        in_specs=[pos_spec, tile_spec if to_sorted else any_spec],
        out_specs=any_spec if to_sorted else tile_spec,
        out_shape=jax.ShapeDtypeStruct((n_out, D_MODEL), F32),
        scratch_shapes=[pltpu.SemaphoreType.DMA],
        compiler_params=pltpu.CompilerParams(dimension_semantics=("arbitrary",), vmem_limit_bytes=VMEM_LIMIT),
    )(pos3, x)


def _moe_kernel(blk_ref, ea_ref, eb_ref, valid_ref, xs_ref, gua_ref, da_ref, gub_ref, db_ref,
                wr_ref, br_ref, l2g_ref, l2b_ref, o_ref):
    g = pl.program_id(0)
    tile = xs_ref.shape[0]
    n_valid = valid_ref[g]

    @pl.when(n_valid > 0)
    def _():
        row = lax.broadcasted_iota(jnp.int32, (tile, 1), 0)
        x = jnp.where(row < n_valid, xs_ref[...], 0.0)
        xb = x.astype(BF16)
        logits = _dot(xb, wr_ref[...]) + br_ref[...]
        lane = lax.broadcasted_iota(jnp.int32, logits.shape, 1)
        gl = jnp.where((lane >= N_EXPERTS) & (lane < N_EXPERTS + N_GROUPS), logits, -jnp.inf)
        g_w = 1.0 / jnp.sum(jnp.exp(gl - jnp.max(gl, axis=1, keepdims=True)), axis=1, keepdims=True)
        l_a = jnp.sum(jnp.where(lane == ea_ref[g], logits, 0.0), axis=1, keepdims=True)
        l_b = jnp.sum(jnp.where(lane == eb_ref[g], logits, 0.0), axis=1, keepdims=True)
        w_a = g_w / (1.0 + jnp.exp(l_b - l_a))
        w_b = g_w / (1.0 + jnp.exp(l_a - l_b))

        def expert(gu_ref, d_ref):
            gu = _dot(xb, gu_ref[0])
            gate, up = gu[:, :D_EXPERT], gu[:, D_EXPERT:]
            return _dot((gate * jax.nn.sigmoid(gate) * up).astype(BF16), d_ref[0])

        y = w_a * expert(gua_ref, da_ref) + w_b * expert(gub_ref, db_ref)
        o_ref[...] = _ln(DEEPNORM_ALPHA * x + y, l2g_ref[...], l2b_ref[...])


def _moe_call(xs, tables, lw, tile):
    _, blk, ea, eb, valid = tables
    n_tiles = blk.shape[0]
    rows = lambda g, blk, ea, eb, valid: (blk[g], 0)
    const = lambda g, blk, ea, eb, valid: (0, 0)
    of_a = lambda g, blk, ea, eb, valid: (ea[g], 0, 0)
    of_b = lambda g, blk, ea, eb, valid: (eb[g], 0, 0)
    gu_shape, d_shape = (1, D_MODEL, 2 * D_EXPERT), (1, D_EXPERT, D_MODEL)
    grid_spec = pltpu.PrefetchScalarGridSpec(
        num_scalar_prefetch=4,
        grid=(n_tiles,),
        in_specs=[pl.BlockSpec((tile, D_MODEL), rows),
                  pl.BlockSpec(gu_shape, of_a), pl.BlockSpec(d_shape, of_a),
                  pl.BlockSpec(gu_shape, of_b), pl.BlockSpec(d_shape, of_b),
                  pl.BlockSpec((D_MODEL, LANES), const), pl.BlockSpec((1, LANES), const),
                  pl.BlockSpec((1, D_MODEL), const), pl.BlockSpec((1, D_MODEL), const)],
        out_specs=pl.BlockSpec((tile, D_MODEL), rows),
    )
    return pl.pallas_call(
        _moe_kernel,
        grid_spec=grid_spec,
        out_shape=jax.ShapeDtypeStruct(xs.shape, F32),
        compiler_params=pltpu.CompilerParams(dimension_semantics=("arbitrary",), vmem_limit_bytes=VMEM_LIMIT),
    )(blk, ea, eb, valid, xs, lw['wegu'], lw['wed'], lw['wegu'], lw['wed'], lw['wrh'], lw['br'],
      lw['l2g'], lw['l2b'])


def _moe(x1, route, cnt, lw, tile, tm):
    T = x1.shape[0]
    n_tiles = T // tile + N_CLASSES
    tables = _class_tables(route, cnt, tile, n_tiles)
    xs = _permute_call(_scatter_kernel, tables[0], x1, n_tiles * tile, tm, True)
    ys = _moe_call(xs, tables, lw, tile)
    return _permute_call(_gather_kernel, tables[0], ys, T, tm, False)


def _rope_tables(S):
    inv_freq = ROPE_THETA ** (-jnp.arange(0, QK_ROPE, 2, dtype=F32) / QK_ROPE)
    ang = jnp.arange(S, dtype=F32)[:, None] * inv_freq[None, :]
    cos, sin = jnp.cos(ang), jnp.sin(ang)
    z = lambda n: jnp.zeros((S, n), F32)
    tail = HEAD_PAD - QK_NOPE - QK_ROPE
    c = jnp.concatenate([jnp.ones((S, QK_NOPE), F32), cos, cos, z(tail)], axis=1)
    s_fwd = jnp.concatenate([z(QK_NOPE + HALF_ROPE), sin, z(tail)], axis=1)
    s_bwd = jnp.concatenate([z(QK_NOPE), -sin, z(HALF_ROPE + tail)], axis=1)
    qs = SOFTMAX_SCALE * math.log2(math.e)
    return ((c * qs).T, (s_fwd * qs).T, (s_bwd * qs).T, c, s_fwd, s_bwd)


def _layer_weights(l, p):
    w_in, b_in = p['w_in'][l], p['b_in'][l]
    rope_lo = GLU_COLS + Q_LORA + KV_LORA
    gate_lo = rope_lo + QK_ROPE
    tail = HEAD_PAD - QK_NOPE - QK_ROPE
    pad_rope = lambda a: jnp.pad(a, ((0, 0), (QK_NOPE, tail)))
    w1 = jnp.concatenate([w_in[:, :rope_lo], pad_rope(w_in[:, rope_lo:gate_lo])], axis=1)
    b1 = jnp.concatenate([b_in[None, :rope_lo], pad_rope(b_in[None, rope_lo:gate_lo])], axis=1)
    wq = jnp.pad(p['w_q_b'][l].reshape(Q_LORA, N_HEADS, QK_NOPE + QK_ROPE), ((0, 0), (0, 0), (0, tail)))
    wkv = p['w_kv_b'][l].reshape(KV_LORA, N_HEADS, QK_NOPE + V_HEAD)
    wk = jnp.pad(wkv[:, :, :QK_NOPE], ((0, 0), (0, 0), (0, HEAD_PAD - QK_NOPE)))
    wv = jnp.pad(wkv[:, :, QK_NOPE:], ((0, 0), (0, 0), (0, V_PAD - V_HEAD)))
    vone = jnp.tile(jnp.arange(V_PAD) == V_HEAD, N_HEADS).astype(F32)[:, None]
    w_r = jnp.pad(jnp.concatenate([p['w_router_expert'][l], p['w_router_group'][l]], axis=1),
                  ((0, 0), (0, LANES - N_EXPERTS - N_GROUPS)))
    b_r = jnp.pad(jnp.concatenate([p['b_router_expert'][l], p['b_router_group'][l]])[None, :],
                  ((0, 0), (0, LANES - N_EXPERTS - N_GROUPS)))
    w_r_hi = w_r.astype(BF16)
    w_r_lo = (w_r - w_r_hi.astype(F32)).astype(BF16)
    r2 = lambda a: a[None, :]
    return dict(
        w1=w1.astype(BF16), b1=b1, qg=r2(p['q_norm_g'][l]), wqt=wq.reshape(Q_LORA, -1).T.astype(BF16),
        kvg=r2(p['kv_norm_g'][l]), wk=wk.reshape(KV_LORA, -1).astype(BF16),
        wvt=wv.reshape(KV_LORA, -1).T.astype(BF16), vone=vone,
        dwk=p['dw_kernel'][l], dwb=r2(p['dw_bias'][l]), clg=r2(p['conv_ln_g'][l]), clb=r2(p['conv_ln_b'][l]),
        wco=p['w_conv_out'][l].astype(BF16), wg=w_in[:, gate_lo:].astype(BF16), bg=b_in[None, gate_lo:],
        wmo=p['w_mla_out'][l].astype(BF16), wo=p['w_out'][l].astype(BF16),
        l1g=r2(p['ln1_g'][l]), l1b=r2(p['ln1_b'][l]), wrh=w_r_hi, wrl=w_r_lo, br=b_r,
        wegu=jnp.concatenate([p['w_gate_e'][l], p['w_up_e'][l]], axis=2).astype(BF16),
        wed=p['w_down_e'][l].astype(BF16),
        l2g=r2(p['ln2_g'][l]), l2b=r2(p['ln2_b'][l]),
    )


def _trunk(x, S, p, *, tm_proj=512, tq=512, tk=1024, tm_post=256, moe_tile=256, tm_perm=512):
    tabs = _rope_tables(S)
    x = _layer_norm_call(x, p['ln_in_g'][None, :], p['ln_in_b'][None, :], tm_proj)
    for l in range(DEPTH):
        lw = _layer_weights(l, p)
        u, qt, k, vt = _proj_call(x, lw, tabs, S, tm_proj)
        o = _attn_call(qt, k, vt, S, tq, tk)
        x1, route, cnt = _post_call(x, u, o, lw, S, tm_post)
        x = _moe(x1, route, cnt, lw, moe_tile, tm_perm)
    return x


def kernel(x_prompt, x_sample, ln_in_g, ln_in_b, w_in, b_in, dw_kernel, dw_bias, conv_ln_g, conv_ln_b, w_conv_out, q_norm_g, w_q_b, kv_norm_g, w_kv_b, w_mla_out, w_out, ln1_g, ln1_b, w_router_group, b_router_group, w_router_expert, b_router_expert, w_gate_e, w_up_e, w_down_e, ln2_g, ln2_b):
    p = dict(ln_in_g=ln_in_g, ln_in_b=ln_in_b, w_in=w_in, b_in=b_in, dw_kernel=dw_kernel, dw_bias=dw_bias,
             conv_ln_g=conv_ln_g, conv_ln_b=conv_ln_b, w_conv_out=w_conv_out, q_norm_g=q_norm_g, w_q_b=w_q_b,
             kv_norm_g=kv_norm_g, w_kv_b=w_kv_b, w_mla_out=w_mla_out, w_out=w_out, ln1_g=ln1_g, ln1_b=ln1_b,
             w_router_group=w_router_group, b_router_group=b_router_group,
             w_router_expert=w_router_expert, b_router_expert=b_router_expert,
             w_gate_e=w_gate_e, w_up_e=w_up_e, w_down_e=w_down_e, ln2_g=ln2_g, ln2_b=ln2_b)
    S = x_prompt.shape[1]
    assert x_sample.shape[1] == S and x_prompt.shape[2] == D_MODEL
    n_prompt = x_prompt.shape[0] * S
    x = jnp.concatenate([x_prompt.reshape(-1, D_MODEL), x_sample.reshape(-1, D_MODEL)], axis=0)
    y = _trunk(x, S, p)
    return (y[:n_prompt].reshape(x_prompt.shape), y[n_prompt:].reshape(x_sample.shape))
```

```python
import functools
import math

import jax
import jax.numpy as jnp
from jax import lax
from jax.experimental import pallas as pl
from jax.experimental.pallas import tpu as pltpu

D_MODEL = 1024
DEPTH = 2
CONV_DIM = 512
CONV_WIDTH = 31
N_HEADS = 8
QK_NOPE = 64
QK_ROPE = 32
V_HEAD = 64
Q_LORA = 384
KV_LORA = 256
ROPE_THETA = 10000.0
SOFTMAX_SCALE = (QK_NOPE + QK_ROPE) ** -0.5
N_GROUPS = 4
EXPERTS_PER_GROUP = 8
N_EXPERTS = N_GROUPS * EXPERTS_PER_GROUP
D_EXPERT = 256
DEEPNORM_ALPHA = (2 * DEPTH) ** 0.25
LN_EPS = 1e-5

LANES = 128
HEAD_PAD = LANES
HALF_ROPE = QK_ROPE // 2
V_PAD = 80
HALO = 16
GLU_COLS = 2 * CONV_DIM
SUBLANES = 8
MXU_COLS = 256
MAX_ROWS = 64
PAIRS_PER_GROUP = EXPERTS_PER_GROUP * (EXPERTS_PER_GROUP - 1) // 2
N_CLASSES = N_GROUPS * PAIRS_PER_GROUP
VMEM_LIMIT = 56 * 1024 * 1024

BF16 = jnp.bfloat16
F32 = jnp.float32


def _ln(x, g, b):
    mu = jnp.mean(x, axis=-1, keepdims=True)
    xc = x - mu
    var = jnp.mean(xc * xc, axis=-1, keepdims=True)
    return xc * lax.rsqrt(var + LN_EPS) * g + b


def _rms(x, g):
    return x * lax.rsqrt(jnp.mean(x * x, axis=-1, keepdims=True) + LN_EPS) * g


def _dot(a, b):
    return jnp.dot(a, b, preferred_element_type=F32)


def _rope(x, c, s_fwd, s_bwd):
    return x * c + pltpu.roll(x, HALF_ROPE, 1) * s_fwd + pltpu.roll(x, HEAD_PAD - HALF_ROPE, 1) * s_bwd


def _rope_t(x, c, s_fwd, s_bwd):
    return x * c + pltpu.roll(x, HALF_ROPE, 0) * s_fwd + pltpu.roll(x, HEAD_PAD - HALF_ROPE, 0) * s_bwd


def _dot_nt(a, b):
    return lax.dot_general(a, b, (((1,), (1,)), ((), ())), preferred_element_type=F32)


def _ln_kernel(xa_ref, xb_ref, g_ref, b_ref, o_ref, *, tiles_a):
    x = jnp.where(pl.program_id(0) < tiles_a, xa_ref[...], xb_ref[...])
    o_ref[...] = _ln(x, g_ref[...], b_ref[...])


def _layer_norm_call(xa, xb, g, b, tm):
    tiles_a, tiles_b = xa.shape[0] // tm, xb.shape[0] // tm
    const = lambda i: (0, 0)
    return pl.pallas_call(
        functools.partial(_ln_kernel, tiles_a=tiles_a),
        grid=(tiles_a + tiles_b,),
        in_specs=[pl.BlockSpec((tm, D_MODEL), lambda i: (jnp.minimum(i, tiles_a - 1), 0)),
                  pl.BlockSpec((tm, D_MODEL), lambda i: (jnp.maximum(i - tiles_a, 0), 0)),
                  pl.BlockSpec((1, D_MODEL), const), pl.BlockSpec((1, D_MODEL), const)],
        out_specs=pl.BlockSpec((tm, D_MODEL), lambda i: (i, 0)),
        out_shape=jax.ShapeDtypeStruct(((tiles_a + tiles_b) * tm, D_MODEL), F32),
        compiler_params=pltpu.CompilerParams(dimension_semantics=("parallel",), vmem_limit_bytes=VMEM_LIMIT),
    )(xa, xb, g, b)


def _proj_kernel(x_ref, w1_ref, b1_ref, qg_ref, wqt_ref, kvg_ref, wk_ref, wvt_ref, vone_ref,
                 cq_ref, sfq_ref, sbq_ref, ck_ref, sfk_ref, sbk_ref,
                 u_ref, qt_ref, k_ref, vt_ref):
    x = x_ref[...].astype(BF16)
    h = _dot(x, w1_ref[...]) + b1_ref[...]
    a = h[:, :CONV_DIM]
    g = h[:, CONV_DIM:GLU_COLS]
    u_ref[...] = a * jax.nn.sigmoid(g)
    qa = h[:, GLU_COLS:GLU_COLS + Q_LORA]
    kva = h[:, GLU_COLS + Q_LORA:GLU_COLS + Q_LORA + KV_LORA]
    kr = h[:, GLU_COLS + Q_LORA + KV_LORA:]

    qt = _dot_nt(wqt_ref[...], _rms(qa, qg_ref[...]).astype(BF16))
    cq, sfq, sbq = cq_ref[...], sfq_ref[...], sbq_ref[...]
    for hd in range(N_HEADS):
        sl = slice(hd * HEAD_PAD, (hd + 1) * HEAD_PAD)
        qt_ref[sl, :] = _rope_t(qt[sl, :], cq, sfq, sbq).astype(BF16)

    kvn = _rms(kva, kvg_ref[...]).astype(BF16)
    kn = _dot(kvn, wk_ref[...])
    krr = _rope(kr, ck_ref[...], sfk_ref[...], sbk_ref[...])
    for hd in range(N_HEADS):
        sl = slice(hd * HEAD_PAD, (hd + 1) * HEAD_PAD)
        k_ref[:, sl] = (kn[:, sl] + krr).astype(BF16)
    vt_ref[...] = (_dot_nt(wvt_ref[...], kvn) + vone_ref[...]).astype(BF16)


def _proj_call(x, lw, tabs, S, tm):
    T = x.shape[0]
    nper = S // tm
    row = lambda i: (i, 0)
    col = lambda i: (0, i)
    const = lambda i: (0, 0)
    full = lambda a: pl.BlockSpec(a.shape, const)
    tab_spec = pl.BlockSpec((tm, HEAD_PAD), lambda i: (i % nper, 0))
    tabt_spec = pl.BlockSpec((HEAD_PAD, tm), lambda i: (0, i % nper))
    wide = N_HEADS * HEAD_PAD
    vrows = N_HEADS * V_PAD
    return pl.pallas_call(
        _proj_kernel,
        grid=(T // tm,),
        in_specs=[pl.BlockSpec((tm, D_MODEL), row), full(lw['w1']), full(lw['b1']), full(lw['qg']),
                  full(lw['wqt']), full(lw['kvg']), full(lw['wk']), full(lw['wvt']), full(lw['vone'])]
                 + [tabt_spec] * 3 + [tab_spec] * 3,
        out_specs=[pl.BlockSpec((tm, CONV_DIM), row), pl.BlockSpec((wide, tm), col),
                   pl.BlockSpec((tm, wide), row), pl.BlockSpec((vrows, tm), col)],
        out_shape=[jax.ShapeDtypeStruct((T, CONV_DIM), F32), jax.ShapeDtypeStruct((wide, T), BF16),
                   jax.ShapeDtypeStruct((T, wide), BF16), jax.ShapeDtypeStruct((vrows, T), BF16)],
        compiler_params=pltpu.CompilerParams(dimension_semantics=("parallel",), vmem_limit_bytes=VMEM_LIMIT),
    )(x, lw['w1'], lw['b1'], lw['qg'], lw['wqt'], lw['kvg'], lw['wk'], lw['wvt'], lw['vone'], *tabs)


def _attn_kernel(qt_ref, k_ref, vt_ref, o_ref, *scratch, tk):
    s_scr = (scratch[0:2], scratch[2:4])
    p_scr = (scratch[4:6], scratch[6:8])
    mx_scr, a_scr = scratch[8:10], scratch[10:12]
    m_scr, acc_scr = scratch[12:]
    n = k_ref.shape[0] // tk

    def key_rows(j):
        return pl.ds(pl.multiple_of(j * tk, tk), tk)

    pieces = [(hh, slice(None)) for hh in range(2)]

    def scores(j, slot, hh, cols):
        sl = slice(hh * HEAD_PAD, (hh + 1) * HEAD_PAD)
        s = _dot(k_ref[key_rows(j), sl], qt_ref[sl, cols])
        s_scr[slot][hh][:, cols] = s
        part = jnp.max(s.reshape(tk // MAX_ROWS, MAX_ROWS, s.shape[1]), axis=0)
        mx_scr[slot][hh, :, cols] = jnp.max(part, axis=0, keepdims=True)

    def softmax(slot, hh, cols):
        m_old = m_scr[hh, :, cols]
        m_new = jnp.maximum(m_old, mx_scr[slot][hh, :, cols])
        p_scr[slot][hh][:, cols] = jnp.exp2(s_scr[slot][hh][:, cols] - m_new).astype(BF16)
        a_scr[slot][hh, :, cols] = jnp.exp2(m_old - m_new)
        m_scr[hh, :, cols] = m_new

    def accum(j, slot, hh, cols):
        vt = vt_ref[hh * V_PAD:(hh + 1) * V_PAD, key_rows(j)]
        acc_scr[hh, :, cols] = (a_scr[slot][hh, :, cols] * acc_scr[hh, :, cols]
                                + _dot(vt, p_scr[slot][hh][:, cols]))

    def stages(score_args=None, softmax_slot=None, accum_args=None):
        for piece in pieces if score_args is not None else ():
            scores(*score_args, *piece)
        for piece in pieces if softmax_slot is not None else ():
            softmax(softmax_slot, *piece)
        for piece in pieces if accum_args is not None else ():
            accum(*accum_args, *piece)

    m_scr[...] = jnp.full(m_scr.shape, -jnp.inf, F32)
    acc_scr[...] = jnp.zeros(acc_scr.shape, F32)
    stages(score_args=(0, 0))
    stages(score_args=(1, 1), softmax_slot=0)

    def body(t, carry):
        j = 2 * t
        stages((j + 2, 0), 1, (j, 0))
        stages((j + 3, 1), 0, (j + 1, 1))
        return carry

    lax.fori_loop(0, n // 2 - 1, body, 0)
    stages(softmax_slot=1, accum_args=(n - 2, 0))
    stages(accum_args=(n - 1, 1))
    ot = jnp.concatenate([acc_scr[hh, :V_HEAD, :] / acc_scr[hh, V_HEAD:V_HEAD + 1, :] for hh in range(2)], axis=0)
    o_ref[...] = ot.T.astype(BF16)


def _attn_call(qt, k, vt, S, tq, tk):
    T = k.shape[0]
    nb = T // S
    nq = S // tq
    assert (S // tk) % 2 == 0
    return pl.pallas_call(
        functools.partial(_attn_kernel, tk=tk),
        grid=(nb, N_HEADS // 2, nq),
        in_specs=[pl.BlockSpec((2 * HEAD_PAD, tq), lambda b, hp, i: (hp, b * nq + i)),
                  pl.BlockSpec((S, 2 * HEAD_PAD), lambda b, hp, i: (b, hp)),
                  pl.BlockSpec((2 * V_PAD, S), lambda b, hp, i: (hp, b))],
        out_specs=pl.BlockSpec((tq, 2 * V_HEAD), lambda b, hp, i: (b * nq + i, hp)),
        out_shape=jax.ShapeDtypeStruct((T, N_HEADS * V_HEAD), BF16),
        scratch_shapes=[pltpu.VMEM((tk, tq), F32)] * 4 + [pltpu.VMEM((tk, tq), BF16)] * 4 + [
                        pltpu.VMEM((2, 1, tq), F32)] * 4 + [
                        pltpu.VMEM((2, 1, tq), F32),
                        pltpu.VMEM((2, V_PAD, tq), F32)],
        compiler_params=pltpu.CompilerParams(dimension_semantics=("parallel", "parallel", "parallel"),
                                             vmem_limit_bytes=VMEM_LIMIT),
    )(qt, k, vt)


def _route(logits):
    lane = lax.broadcasted_iota(jnp.int32, logits.shape, 1)
    neg = -jnp.inf
    is_group = (lane >= N_EXPERTS) & (lane < N_EXPERTS + N_GROUPS)
    gl = jnp.where(is_group, logits, neg)
    gmax = jnp.max(gl, axis=1, keepdims=True)
    g_w = 1.0 / jnp.sum(jnp.exp(gl - gmax), axis=1, keepdims=True)
    g_idx = jnp.min(jnp.where(gl == gmax, lane, 2 * LANES), axis=1, keepdims=True) - N_EXPERTS
    lo = g_idx * EXPERTS_PER_GROUP
    el = jnp.where((lane >= lo) & (lane < lo + EXPERTS_PER_GROUP), logits, neg)
    m1 = jnp.max(el, axis=1, keepdims=True)
    i1 = jnp.min(jnp.where(el == m1, lane, 2 * LANES), axis=1, keepdims=True)
    el2 = jnp.where(lane == i1, neg, el)
    m2 = jnp.max(el2, axis=1, keepdims=True)
    i2 = jnp.min(jnp.where(el2 == m2, lane, 2 * LANES), axis=1, keepdims=True)
    r = jnp.exp(m2 - m1)
    w1 = g_w / (1.0 + r)
    w2 = g_w * r / (1.0 + r)
    return g_idx, i1, i2, w1, w2, lane


def _post_kernel(x_ref, u_ref, up_ref, un_ref, o_ref,
                 dwk_ref, dwb_ref, clg_ref, clb_ref, wco_ref, wg_ref, bg_ref, wmo_ref, wo_ref,
                 l1g_ref, l1b_ref, wrh_ref, wrl_ref, br_ref,
                 x1_ref, route_ref, cnt_ref, ubuf, ush, cnt_scr, *, nper):
    tm = x_ref.shape[0]
    i = pl.program_id(0)
    first = (i % nper) == 0
    last = (i % nper) == nper - 1
    ubuf[0:HALO, :] = jnp.where(first, 0.0, up_ref[...])
    ubuf[HALO:HALO + tm, :] = u_ref[...]
    ubuf[HALO + tm:, :] = jnp.where(last, 0.0, un_ref[...])
    conv = jnp.zeros((tm, CONV_DIM), F32) + dwb_ref[...]
    off = HALO - CONV_WIDTH // 2
    span = ush.shape[1]
    for b in range(SUBLANES):
        ush[b] = ubuf[off + b:off + b + span, :]
    for b in range(SUBLANES):
        for w in range(b, CONV_WIDTH, SUBLANES):
            conv = conv + ush[b, w - b:w - b + tm, :] * dwk_ref[w:w + 1, :]
    c = _ln(conv, clg_ref[...], clb_ref[...])
    c = c * jax.nn.sigmoid(c)
    conv_out = _dot(c.astype(BF16), wco_ref[...])

    x = x_ref[...]
    gates = jax.nn.sigmoid(_dot(x.astype(BF16), wg_ref[...]) + bg_ref[...])
    mla_out = _dot(o_ref[...], wmo_ref[...])
    merged = gates[:, :D_MODEL] * conv_out + gates[:, D_MODEL:] * mla_out
    m = _dot(merged.astype(BF16), wo_ref[...])
    x1 = _ln(DEEPNORM_ALPHA * x + m, l1g_ref[...], l1b_ref[...])
    x1_ref[...] = x1

    hi = x1.astype(BF16)
    lo = (x1 - hi.astype(F32)).astype(BF16)
    logits = _dot(hi, wrh_ref[...]) + _dot(lo, wrh_ref[...]) + _dot(hi, wrl_ref[...]) + br_ref[...]
    g_idx, i1, i2, _, _, lane = _route(logits)
    la = jnp.minimum(i1, i2) - g_idx * EXPERTS_PER_GROUP
    lb = jnp.maximum(i1, i2) - g_idx * EXPERTS_PER_GROUP
    pair = jnp.right_shift(la * (2 * EXPERTS_PER_GROUP - 1 - la), 1) + (lb - la - 1)
    cls = g_idx * PAIRS_PER_GROUP + pair
    onehot = lane == cls

    @pl.when(i == 0)
    def _():
        cnt_scr[...] = jnp.zeros_like(cnt_scr)

    r_id = lax.broadcasted_iota(jnp.int32, (tm, tm), 0)
    c_id = lax.broadcasted_iota(jnp.int32, (tm, tm), 1)
    earlier = _dot((c_id < r_id).astype(BF16), onehot.astype(BF16))
    rank = jnp.sum(jnp.where(onehot, earlier + cnt_scr[...], 0.0), axis=1, keepdims=True)
    cnt_scr[...] += jnp.sum(onehot.astype(F32), axis=0, keepdims=True)
    cnt_ref[...] = cnt_scr[...]
    route_ref[...] = jnp.where(lane == 0, cls.astype(F32), jnp.where(lane == 1, rank, 0.0))


def _post_call(x, u, o, lw, S, tm):
    T = x.shape[0]
    nper = S // tm
    nh = tm // HALO
    last_halo = T // HALO - 1
    row = lambda i: (i, 0)
    const = lambda i: (0, 0)
    full = lambda a: pl.BlockSpec(a.shape, const)
    names = ['dwk', 'dwb', 'clg', 'clb', 'wco', 'wg', 'bg', 'wmo', 'wo', 'l1g', 'l1b', 'wrh', 'wrl', 'br']
    ws = [lw[n] for n in names]
    return pl.pallas_call(
        functools.partial(_post_kernel, nper=nper),
        grid=(T // tm,),
        in_specs=[pl.BlockSpec((tm, D_MODEL), row), pl.BlockSpec((tm, CONV_DIM), row),
                  pl.BlockSpec((HALO, CONV_DIM), lambda i: (jnp.maximum(i * nh - 1, 0), 0)),
                  pl.BlockSpec((HALO, CONV_DIM), lambda i: (jnp.minimum((i + 1) * nh, last_halo), 0)),
                  pl.BlockSpec((tm, N_HEADS * V_HEAD), row)] + [full(a) for a in ws],
        out_specs=[pl.BlockSpec((tm, D_MODEL), row), pl.BlockSpec((tm, LANES), row),
                   pl.BlockSpec((1, LANES), const)],
        out_shape=[jax.ShapeDtypeStruct((T, D_MODEL), F32), jax.ShapeDtypeStruct((T, LANES), F32),
                   jax.ShapeDtypeStruct((1, LANES), F32)],
        scratch_shapes=[pltpu.VMEM((tm + 2 * HALO, CONV_DIM), F32),
                        pltpu.VMEM((SUBLANES, tm + (CONV_WIDTH - 1) // SUBLANES * SUBLANES, CONV_DIM), F32),
                        pltpu.VMEM((1, LANES), F32)],
        compiler_params=pltpu.CompilerParams(dimension_semantics=("arbitrary",), vmem_limit_bytes=VMEM_LIMIT),
    )(x, u, u, u, o, *ws)


def _class_tables(route, cnt, tile, n_tiles):
    cls = route[:, 0].astype(jnp.int32)
    rank = route[:, 1].astype(jnp.int32)
    counts = cnt[0, :N_CLASSES].astype(jnp.int32)
    tiles_of = (counts + tile - 1) // tile
    t_end = jnp.cumsum(tiles_of)
    t_start = t_end - tiles_of
    class_ids = jnp.arange(N_CLASSES, dtype=jnp.int32)[None, :]

    def lookup(idx, table):
        return jnp.sum(jnp.where(idx[:, None] == class_ids, table[None, :], 0), axis=1)

    pos = lookup(cls, t_start * tile) + rank
    g = jnp.arange(n_tiles, dtype=jnp.int32)
    blk = jnp.minimum(g, t_end[-1] - 1)
    c = jnp.sum((blk[:, None] >= t_end[None, :]).astype(jnp.int32), axis=1)
    first_row = (blk - lookup(c, t_start)) * tile
    valid = jnp.where(g < t_end[-1], jnp.clip(lookup(c, counts) - first_row, 0, tile), 0)
    pairs = [(a, b) for a in range(EXPERTS_PER_GROUP) for b in range(a + 1, EXPERTS_PER_GROUP)]
    group_lo = (class_ids[0] // PAIRS_PER_GROUP) * EXPERTS_PER_GROUP
    ea = lookup(c, group_lo + jnp.asarray([a for a, _ in pairs] * N_GROUPS, jnp.int32))
    eb = lookup(c, group_lo + jnp.asarray([b for _, b in pairs] * N_GROUPS, jnp.int32))
    return pos, blk, ea, eb, valid.astype(jnp.int32)


def _row_copy(src_ref, src_row, dst_ref, dst_row, sem):
    return pltpu.make_async_copy(src_ref.at[pl.ds(src_row, 1)], dst_ref.at[pl.ds(dst_row, 1)], sem)


def _scatter_kernel(pos_ref, x_ref, xs_ref, sem):
    tm = x_ref.shape[0]

    def start(i, carry):
        base = pl.multiple_of(i * SUBLANES, SUBLANES)
        for u in range(SUBLANES):
            _row_copy(x_ref, base + u, xs_ref, pos_ref[0, 0, base + u], sem).start()
        return carry

    lax.fori_loop(0, tm // SUBLANES, start, 0)
    pltpu.make_async_copy(x_ref, xs_ref.at[pl.ds(0, tm)], sem).wait()


def _gather_kernel(pos_ref, xs_ref, o_ref, sem):
    tm = o_ref.shape[0]

    def start(i, carry):
        base = pl.multiple_of(i * SUBLANES, SUBLANES)
        for u in range(SUBLANES):
            _row_copy(xs_ref, pos_ref[0, 0, base + u], o_ref, base + u, sem).start()
        return carry

    lax.fori_loop(0, tm // SUBLANES, start, 0)
    pltpu.make_async_copy(xs_ref.at[pl.ds(0, tm)], o_ref, sem).wait()


def _permute_call(kernel_fn, pos, x, n_out, tm, to_sorted):
    T = pos.shape[0]
    pos3 = pos.reshape(T // tm, 1, tm)
    pos_spec = pl.BlockSpec((1, 1, tm), lambda i: (i, 0, 0), memory_space=pltpu.SMEM)
    tile_spec = pl.BlockSpec((tm, D_MODEL), lambda i: (i, 0))
    any_spec = pl.BlockSpec(memory_space=pl.ANY)
    return pl.pallas_call(
        kernel_fn,
        grid=(T // tm,),
        in_specs=[pos_spec, tile_spec if to_sorted else any_spec],
        out_specs=any_spec if to_sorted else tile_spec,
        out_shape=jax.ShapeDtypeStruct((n_out, D_MODEL), F32),
        scratch_shapes=[pltpu.SemaphoreType.DMA],
        compiler_params=pltpu.CompilerParams(dimension_semantics=("arbitrary",), vmem_limit_bytes=VMEM_LIMIT),
    )(pos3, x)


def _moe_kernel(blk_ref, ea_ref, eb_ref, valid_ref, xs_ref, gua_ref, da_ref, gub_ref, db_ref,
                wr_ref, br_ref, l2g_ref, l2b_ref, o_ref):
    g = pl.program_id(0)
    tile = xs_ref.shape[0]
    n_valid = valid_ref[g]

    @pl.when(n_valid > 0)
    def _():
        row = lax.broadcasted_iota(jnp.int32, (tile, 1), 0)
        x = jnp.where(row < n_valid, xs_ref[...], 0.0)
        xb = x.astype(BF16)
        logits = _dot(xb, wr_ref[...]) + br_ref[...]
        lane = lax.broadcasted_iota(jnp.int32, logits.shape, 1)
        gl = jnp.where((lane >= N_EXPERTS) & (lane < N_EXPERTS + N_GROUPS), logits, -jnp.inf)
        g_w = 1.0 / jnp.sum(jnp.exp(gl - jnp.max(gl, axis=1, keepdims=True)), axis=1, keepdims=True)
        l_a = jnp.sum(jnp.where(lane == ea_ref[g], logits, 0.0), axis=1, keepdims=True)
        l_b = jnp.sum(jnp.where(lane == eb_ref[g], logits, 0.0), axis=1, keepdims=True)
        w_a = g_w / (1.0 + jnp.exp(l_b - l_a))
        w_b = g_w / (1.0 + jnp.exp(l_a - l_b))

        def expert(gu_ref, d_ref):
            gu = _dot(xb, gu_ref[0])
            gate, up = gu[:, :D_EXPERT], gu[:, D_EXPERT:]
            return _dot((gate * jax.nn.sigmoid(gate) * up).astype(BF16), d_ref[0])

        y = w_a * expert(gua_ref, da_ref) + w_b * expert(gub_ref, db_ref)
        o_ref[...] = _ln(DEEPNORM_ALPHA * x + y, l2g_ref[...], l2b_ref[...])


def _moe_call(xs, tables, lw, tile):
    _, blk, ea, eb, valid = tables
    n_tiles = blk.shape[0]
    rows = lambda g, blk, ea, eb, valid: (blk[g], 0)
    const = lambda g, blk, ea, eb, valid: (0, 0)
    of_a = lambda g, blk, ea, eb, valid: (ea[g], 0, 0)
    of_b = lambda g, blk, ea, eb, valid: (eb[g], 0, 0)
    gu_shape, d_shape = (1, D_MODEL, 2 * D_EXPERT), (1, D_EXPERT, D_MODEL)
    grid_spec = pltpu.PrefetchScalarGridSpec(
        num_scalar_prefetch=4,
        grid=(n_tiles,),
        in_specs=[pl.BlockSpec((tile, D_MODEL), rows),
                  pl.BlockSpec(gu_shape, of_a), pl.BlockSpec(d_shape, of_a),
                  pl.BlockSpec(gu_shape, of_b), pl.BlockSpec(d_shape, of_b),
                  pl.BlockSpec((D_MODEL, LANES), const), pl.BlockSpec((1, LANES), const),
                  pl.BlockSpec((1, D_MODEL), const), pl.BlockSpec((1, D_MODEL), const)],
        out_specs=pl.BlockSpec((tile, D_MODEL), rows),
    )
    return pl.pallas_call(
        _moe_kernel,
        grid_spec=grid_spec,
        out_shape=jax.ShapeDtypeStruct(xs.shape, F32),
        compiler_params=pltpu.CompilerParams(dimension_semantics=("arbitrary",), vmem_limit_bytes=VMEM_LIMIT),
    )(blk, ea, eb, valid, xs, lw['wegu'], lw['wed'], lw['wegu'], lw['wed'], lw['wrh'], lw['br'],
      lw['l2g'], lw['l2b'])


def _moe(x1, route, cnt, lw, tile, tm, row_splits):
    T = x1.shape[0]
    n_tiles = T // tile + N_CLASSES
    tables = _class_tables(route, cnt, tile, n_tiles)
    pos = tables[0]
    xs = _permute_call(_scatter_kernel, pos, x1, n_tiles * tile, tm, True)
    ys = _moe_call(xs, tables, lw, tile)
    outs, lo = [], 0
    for n in row_splits:
        outs.append(_permute_call(_gather_kernel, pos[lo:lo + n], ys, n, tm, False))
        lo += n
    return outs


def _rope_tables(S):
    inv_freq = ROPE_THETA ** (-jnp.arange(0, QK_ROPE, 2, dtype=F32) / QK_ROPE)
    ang = jnp.arange(S, dtype=F32)[:, None] * inv_freq[None, :]
    cos, sin = jnp.cos(ang), jnp.sin(ang)
    z = lambda n: jnp.zeros((S, n), F32)
    tail = HEAD_PAD - QK_NOPE - QK_ROPE
    c = jnp.concatenate([jnp.ones((S, QK_NOPE), F32), cos, cos, z(tail)], axis=1)
    s_fwd = jnp.concatenate([z(QK_NOPE + HALF_ROPE), sin, z(tail)], axis=1)
    s_bwd = jnp.concatenate([z(QK_NOPE), -sin, z(HALF_ROPE + tail)], axis=1)
    qs = SOFTMAX_SCALE * math.log2(math.e)
    return ((c * qs).T, (s_fwd * qs).T, (s_bwd * qs).T, c, s_fwd, s_bwd)


def _layer_weights(l, p):
    w_in, b_in = p['w_in'][l], p['b_in'][l]
    rope_lo = GLU_COLS + Q_LORA + KV_LORA
    gate_lo = rope_lo + QK_ROPE
    tail = HEAD_PAD - QK_NOPE - QK_ROPE
    pad_rope = lambda a: jnp.pad(a, ((0, 0), (QK_NOPE, tail)))
    w1 = jnp.concatenate([w_in[:, :rope_lo], pad_rope(w_in[:, rope_lo:gate_lo])], axis=1)
    b1 = jnp.concatenate([b_in[None, :rope_lo], pad_rope(b_in[None, rope_lo:gate_lo])], axis=1)
    wq = jnp.pad(p['w_q_b'][l].reshape(Q_LORA, N_HEADS, QK_NOPE + QK_ROPE), ((0, 0), (0, 0), (0, tail)))
    wkv = p['w_kv_b'][l].reshape(KV_LORA, N_HEADS, QK_NOPE + V_HEAD)
    wk = jnp.pad(wkv[:, :, :QK_NOPE], ((0, 0), (0, 0), (0, HEAD_PAD - QK_NOPE)))
    wv = jnp.pad(wkv[:, :, QK_NOPE:], ((0, 0), (0, 0), (0, V_PAD - V_HEAD)))
    vone = jnp.tile(jnp.arange(V_PAD) == V_HEAD, N_HEADS).astype(F32)[:, None]
    w_r = jnp.pad(jnp.concatenate([p['w_router_expert'][l], p['w_router_group'][l]], axis=1),
                  ((0, 0), (0, LANES - N_EXPERTS - N_GROUPS)))
    b_r = jnp.pad(jnp.concatenate([p['b_router_expert'][l], p['b_router_group'][l]])[None, :],
                  ((0, 0), (0, LANES - N_EXPERTS - N_GROUPS)))
    w_r_hi = w_r.astype(BF16)
    w_r_lo = (w_r - w_r_hi.astype(F32)).astype(BF16)
    r2 = lambda a: a[None, :]
    return dict(
        w1=w1.astype(BF16), b1=b1, qg=r2(p['q_norm_g'][l]), wqt=wq.reshape(Q_LORA, -1).T.astype(BF16),
        kvg=r2(p['kv_norm_g'][l]), wk=wk.reshape(KV_LORA, -1).astype(BF16),
        wvt=wv.reshape(KV_LORA, -1).T.astype(BF16), vone=vone,
        dwk=p['dw_kernel'][l], dwb=r2(p['dw_bias'][l]), clg=r2(p['conv_ln_g'][l]), clb=r2(p['conv_ln_b'][l]),
        wco=p['w_conv_out'][l].astype(BF16), wg=w_in[:, gate_lo:].astype(BF16), bg=b_in[None, gate_lo:],
        wmo=p['w_mla_out'][l].astype(BF16), wo=p['w_out'][l].astype(BF16),
        l1g=r2(p['ln1_g'][l]), l1b=r2(p['ln1_b'][l]), wrh=w_r_hi, wrl=w_r_lo, br=b_r,
        wegu=jnp.concatenate([p['w_gate_e'][l], p['w_up_e'][l]], axis=2).astype(BF16),
        wed=p['w_down_e'][l].astype(BF16),
        l2g=r2(p['ln2_g'][l]), l2b=r2(p['ln2_b'][l]),
    )


def _trunk(xa, xb, S, p, *, tm_proj=512, tq=512, tk=1024, tm_post=512, moe_tile=256, tm_perm=512):
    tabs = _rope_tables(S)
    T = xa.shape[0] + xb.shape[0]
    x = _layer_norm_call(xa, xb, p['ln_in_g'][None, :], p['ln_in_b'][None, :], tm_proj)
    for l in range(DEPTH):
        lw = _layer_weights(l, p)
        u, qt, k, vt = _proj_call(x, lw, tabs, S, tm_proj)
        o = _attn_call(qt, k, vt, S, tq, tk)
        x1, route, cnt = _post_call(x, u, o, lw, S, tm_post)
        splits = (xa.shape[0], xb.shape[0]) if l == DEPTH - 1 else (T,)
        outs = _moe(x1, route, cnt, lw, moe_tile, tm_perm, splits)
        x = outs[0]
    return outs


def kernel(x_prompt, x_sample, ln_in_g, ln_in_b, w_in, b_in, dw_kernel, dw_bias, conv_ln_g, conv_ln_b, w_conv_out, q_norm_g, w_q_b, kv_norm_g, w_kv_b, w_mla_out, w_out, ln1_g, ln1_b, w_router_group, b_router_group, w_router_expert, b_router_expert, w_gate_e, w_up_e, w_down_e, ln2_g, ln2_b):
    p = dict(ln_in_g=ln_in_g, ln_in_b=ln_in_b, w_in=w_in, b_in=b_in, dw_kernel=dw_kernel, dw_bias=dw_bias,
             conv_ln_g=conv_ln_g, conv_ln_b=conv_ln_b, w_conv_out=w_conv_out, q_norm_g=q_norm_g, w_q_b=w_q_b,
             kv_norm_g=kv_norm_g, w_kv_b=w_kv_b, w_mla_out=w_mla_out, w_out=w_out, ln1_g=ln1_g, ln1_b=ln1_b,
             w_router_group=w_router_group, b_router_group=b_router_group,
             w_router_expert=w_router_expert, b_router_expert=b_router_expert,
             w_gate_e=w_gate_e, w_up_e=w_up_e, w_down_e=w_down_e, ln2_g=ln2_g, ln2_b=ln2_b)
    S = x_prompt.shape[1]
    assert x_sample.shape[1] == S and x_prompt.shape[2] == D_MODEL
    y_prompt, y_sample = _trunk(x_prompt.reshape(-1, D_MODEL), x_sample.reshape(-1, D_MODEL), S, p)
    return (y_prompt.reshape(x_prompt.shape), y_sample.reshape(x_sample.shape))
```

```python
import functools
import math

import jax
import jax.numpy as jnp
from jax import lax
from jax.experimental import pallas as pl
from jax.experimental.pallas import tpu as pltpu

D_MODEL = 1024
DEPTH = 2
CONV_DIM = 512
CONV_WIDTH = 31
N_HEADS = 8
QK_NOPE = 64
QK_ROPE = 32
V_HEAD = 64
Q_LORA = 384
KV_LORA = 256
ROPE_THETA = 10000.0
SOFTMAX_SCALE = (QK_NOPE + QK_ROPE) ** -0.5
N_GROUPS = 4
EXPERTS_PER_GROUP = 8
N_EXPERTS = N_GROUPS * EXPERTS_PER_GROUP
D_EXPERT = 256
DEEPNORM_ALPHA = (2 * DEPTH) ** 0.25
LN_EPS = 1e-5

LANES = 128
HEAD_PAD = LANES
HALF_ROPE = QK_ROPE // 2
V_PAD = 80
HALO = 16
GLU_COLS = 2 * CONV_DIM
SUBLANES = 8
MXU_COLS = 256
MAX_ROWS = 64
PAIRS_PER_GROUP = EXPERTS_PER_GROUP * (EXPERTS_PER_GROUP - 1) // 2
N_CLASSES = N_GROUPS * PAIRS_PER_GROUP
VMEM_LIMIT = 56 * 1024 * 1024

BF16 = jnp.bfloat16
F32 = jnp.float32


def _ln(x, g, b):
    mu = jnp.mean(x, axis=-1, keepdims=True)
    xc = x - mu
    var = jnp.mean(xc * xc, axis=-1, keepdims=True)
    return xc * lax.rsqrt(var + LN_EPS) * g + b


def _rms(x, g):
    return x * lax.rsqrt(jnp.mean(x * x, axis=-1, keepdims=True) + LN_EPS) * g


def _dot(a, b):
    return jnp.dot(a, b, preferred_element_type=F32)


def _rope(x, c, s_fwd, s_bwd):
    return x * c + pltpu.roll(x, HALF_ROPE, 1) * s_fwd + pltpu.roll(x, HEAD_PAD - HALF_ROPE, 1) * s_bwd


def _rope_t(x, c, s_fwd, s_bwd):
    return x * c + pltpu.roll(x, HALF_ROPE, 0) * s_fwd + pltpu.roll(x, HEAD_PAD - HALF_ROPE, 0) * s_bwd


def _dot_nt(a, b):
    return lax.dot_general(a, b, (((1,), (1,)), ((), ())), preferred_element_type=F32)


def _ln_kernel(xa_ref, xb_ref, g_ref, b_ref, o_ref, *, tiles_a):
    x = jnp.where(pl.program_id(0) < tiles_a, xa_ref[...], xb_ref[...])
    o_ref[...] = _ln(x, g_ref[...], b_ref[...])


def _layer_norm_call(xa, xb, g, b, tm):
    tiles_a, tiles_b = xa.shape[0] // tm, xb.shape[0] // tm
    const = lambda i: (0, 0)
    return pl.pallas_call(
        functools.partial(_ln_kernel, tiles_a=tiles_a),
        grid=(tiles_a + tiles_b,),
        in_specs=[pl.BlockSpec((tm, D_MODEL), lambda i: (jnp.minimum(i, tiles_a - 1), 0)),
                  pl.BlockSpec((tm, D_MODEL), lambda i: (jnp.maximum(i - tiles_a, 0), 0)),
                  pl.BlockSpec((1, D_MODEL), const), pl.BlockSpec((1, D_MODEL), const)],
        out_specs=pl.BlockSpec((tm, D_MODEL), lambda i: (i, 0)),
        out_shape=jax.ShapeDtypeStruct(((tiles_a + tiles_b) * tm, D_MODEL), F32),
        compiler_params=pltpu.CompilerParams(dimension_semantics=("parallel",), vmem_limit_bytes=VMEM_LIMIT),
    )(xa, xb, g, b)


def _proj_kernel(x_ref, w1_ref, b1_ref, qg_ref, wqt_ref, kvg_ref, wk_ref, wvt_ref, vone_ref,
                 cq_ref, sfq_ref, sbq_ref, ck_ref, sfk_ref, sbk_ref,
                 u_ref, qt_ref, k_ref, vt_ref):
    x = x_ref[...].astype(BF16)
    h = _dot(x, w1_ref[...]) + b1_ref[...]
    a = h[:, :CONV_DIM]
    g = h[:, CONV_DIM:GLU_COLS]
    u_ref[...] = a * jax.nn.sigmoid(g)
    qa = h[:, GLU_COLS:GLU_COLS + Q_LORA]
    kva = h[:, GLU_COLS + Q_LORA:GLU_COLS + Q_LORA + KV_LORA]
    kr = h[:, GLU_COLS + Q_LORA + KV_LORA:]

    qt = _dot_nt(wqt_ref[...], _rms(qa, qg_ref[...]).astype(BF16))
    cq, sfq, sbq = cq_ref[...], sfq_ref[...], sbq_ref[...]
    for hd in range(N_HEADS):
        sl = slice(hd * HEAD_PAD, (hd + 1) * HEAD_PAD)
        qt_ref[sl, :] = _rope_t(qt[sl, :], cq, sfq, sbq).astype(BF16)

    kvn = _rms(kva, kvg_ref[...]).astype(BF16)
    kn = _dot(kvn, wk_ref[...])
    krr = _rope(kr, ck_ref[...], sfk_ref[...], sbk_ref[...])
    for hd in range(N_HEADS):
        sl = slice(hd * HEAD_PAD, (hd + 1) * HEAD_PAD)
        k_ref[:, sl] = (kn[:, sl] + krr).astype(BF16)
    vt_ref[...] = (_dot_nt(wvt_ref[...], kvn) + vone_ref[...]).astype(BF16)


def _proj_call(x, lw, tabs, S, tm):
    T = x.shape[0]
    nper = S // tm
    row = lambda i: (i, 0)
    col = lambda i: (0, i)
    const = lambda i: (0, 0)
    full = lambda a: pl.BlockSpec(a.shape, const)
    tab_spec = pl.BlockSpec((tm, HEAD_PAD), lambda i: (i % nper, 0))
    tabt_spec = pl.BlockSpec((HEAD_PAD, tm), lambda i: (0, i % nper))
    wide = N_HEADS * HEAD_PAD
    vrows = N_HEADS * V_PAD
    return pl.pallas_call(
        _proj_kernel,
        grid=(T // tm,),
        in_specs=[pl.BlockSpec((tm, D_MODEL), row), full(lw['w1']), full(lw['b1']), full(lw['qg']),
                  full(lw['wqt']), full(lw['kvg']), full(lw['wk']), full(lw['wvt']), full(lw['vone'])]
                 + [tabt_spec] * 3 + [tab_spec] * 3,
        out_specs=[pl.BlockSpec((tm, CONV_DIM), row), pl.BlockSpec((wide, tm), col),
                   pl.BlockSpec((tm, wide), row), pl.BlockSpec((vrows, tm), col)],
        out_shape=[jax.ShapeDtypeStruct((T, CONV_DIM), F32), jax.ShapeDtypeStruct((wide, T), BF16),
                   jax.ShapeDtypeStruct((T, wide), BF16), jax.ShapeDtypeStruct((vrows, T), BF16)],
        compiler_params=pltpu.CompilerParams(dimension_semantics=("parallel",), vmem_limit_bytes=VMEM_LIMIT),
    )(x, lw['w1'], lw['b1'], lw['qg'], lw['wqt'], lw['kvg'], lw['wk'], lw['wvt'], lw['vone'], *tabs)


def _attn_kernel(qt_ref, k_ref, vt_ref, o_ref, *scratch, tk):
    s_scr = (scratch[0:2], scratch[2:4])
    p_scr = (scratch[4:6], scratch[6:8])
    mx_scr, a_scr = scratch[8:10], scratch[10:12]
    m_scr, acc_scr = scratch[12:]
    n = k_ref.shape[0] // tk

    def key_rows(j):
        return pl.ds(pl.multiple_of(j * tk, tk), tk)

    pieces = [(hh, slice(None)) for hh in range(2)]

    def scores(j, slot, hh, cols):
        sl = slice(hh * HEAD_PAD, (hh + 1) * HEAD_PAD)
        s = _dot(k_ref[key_rows(j), sl], qt_ref[sl, cols])
        s_scr[slot][hh][:, cols] = s
        part = jnp.max(s.reshape(tk // MAX_ROWS, MAX_ROWS, s.shape[1]), axis=0)
        mx_scr[slot][hh, :, cols] = jnp.max(part, axis=0, keepdims=True)

    def softmax(slot, hh, cols):
        m_old = m_scr[hh, :, cols]
        m_new = jnp.maximum(m_old, mx_scr[slot][hh, :, cols])
        p_scr[slot][hh][:, cols] = jnp.exp2(s_scr[slot][hh][:, cols] - m_new).astype(BF16)
        a_scr[slot][hh, :, cols] = jnp.exp2(m_old - m_new)
        m_scr[hh, :, cols] = m_new

    def accum(j, slot, hh, cols):
        vt = vt_ref[hh * V_PAD:(hh + 1) * V_PAD, key_rows(j)]
        acc_scr[hh, :, cols] = (a_scr[slot][hh, :, cols] * acc_scr[hh, :, cols]
                                + _dot(vt, p_scr[slot][hh][:, cols]))

    def stages(score_args=None, softmax_slot=None, accum_args=None):
        for piece in pieces if score_args is not None else ():
            scores(*score_args, *piece)
        for piece in pieces if softmax_slot is not None else ():
            softmax(softmax_slot, *piece)
        for piece in pieces if accum_args is not None else ():
            accum(*accum_args, *piece)

    m_scr[...] = jnp.full(m_scr.shape, -jnp.inf, F32)
    acc_scr[...] = jnp.zeros(acc_scr.shape, F32)
    stages(score_args=(0, 0))
    stages(score_args=(1, 1), softmax_slot=0)

    def body(t, carry):
        j = 2 * t
        stages((j + 2, 0), 1, (j, 0))
        stages((j + 3, 1), 0, (j + 1, 1))
        return carry

    lax.fori_loop(0, n // 2 - 1, body, 0)
    stages(softmax_slot=1, accum_args=(n - 2, 0))
    stages(accum_args=(n - 1, 1))
    ot = jnp.concatenate([acc_scr[hh, :V_HEAD, :] / acc_scr[hh, V_HEAD:V_HEAD + 1, :] for hh in range(2)], axis=0)
    o_ref[...] = ot.T.astype(BF16)


def _attn_call(qt, k, vt, S, tq, tk):
    T = k.shape[0]
    nb = T // S
    nq = S // tq
    assert (S // tk) % 2 == 0
    return pl.pallas_call(
        functools.partial(_attn_kernel, tk=tk),
        grid=(nb, N_HEADS // 2, nq),
        in_specs=[pl.BlockSpec((2 * HEAD_PAD, tq), lambda b, hp, i: (hp, b * nq + i)),
                  pl.BlockSpec((S, 2 * HEAD_PAD), lambda b, hp, i: (b, hp)),
                  pl.BlockSpec((2 * V_PAD, S), lambda b, hp, i: (hp, b))],
        out_specs=pl.BlockSpec((tq, 2 * V_HEAD), lambda b, hp, i: (b * nq + i, hp)),
        out_shape=jax.ShapeDtypeStruct((T, N_HEADS * V_HEAD), BF16),
        scratch_shapes=[pltpu.VMEM((tk, tq), F32)] * 4 + [pltpu.VMEM((tk, tq), BF16)] * 4 + [
                        pltpu.VMEM((2, 1, tq), F32)] * 4 + [
                        pltpu.VMEM((2, 1, tq), F32),
                        pltpu.VMEM((2, V_PAD, tq), F32)],
        compiler_params=pltpu.CompilerParams(dimension_semantics=("parallel", "parallel", "parallel"),
                                             vmem_limit_bytes=VMEM_LIMIT),
    )(qt, k, vt)


def _route(logits):
    lane = lax.broadcasted_iota(jnp.int32, logits.shape, 1)
    neg = -jnp.inf
    is_group = (lane >= N_EXPERTS) & (lane < N_EXPERTS + N_GROUPS)
    gl = jnp.where(is_group, logits, neg)
    gmax = jnp.max(gl, axis=1, keepdims=True)
    g_w = 1.0 / jnp.sum(jnp.exp(gl - gmax), axis=1, keepdims=True)
    g_idx = jnp.min(jnp.where(gl == gmax, lane, 2 * LANES), axis=1, keepdims=True) - N_EXPERTS
    lo = g_idx * EXPERTS_PER_GROUP
    el = jnp.where((lane >= lo) & (lane < lo + EXPERTS_PER_GROUP), logits, neg)
    m1 = jnp.max(el, axis=1, keepdims=True)
    i1 = jnp.min(jnp.where(el == m1, lane, 2 * LANES), axis=1, keepdims=True)
    el2 = jnp.where(lane == i1, neg, el)
    m2 = jnp.max(el2, axis=1, keepdims=True)
    i2 = jnp.min(jnp.where(el2 == m2, lane, 2 * LANES), axis=1, keepdims=True)
    r = jnp.exp(m2 - m1)
    w1 = g_w / (1.0 + r)
    w2 = g_w * r / (1.0 + r)
    return g_idx, i1, i2, w1, w2, lane


def _post_kernel(x_ref, u_ref, up_ref, un_ref, o_ref,
                 dwk_ref, dwb_ref, clg_ref, clb_ref, wco_ref, wg_ref, bg_ref, wmo_ref, wo_ref,
                 l1g_ref, l1b_ref, wrh_ref, wrl_ref, br_ref,
                 x1_ref, route_ref, cnt_ref, ubuf, ush, cnt_scr, *, nper):
    tm = x_ref.shape[0]
    i = pl.program_id(0)
    first = (i % nper) == 0
    last = (i % nper) == nper - 1
    ubuf[0:HALO, :] = jnp.where(first, 0.0, up_ref[...])
    ubuf[HALO:HALO + tm, :] = u_ref[...]
    ubuf[HALO + tm:, :] = jnp.where(last, 0.0, un_ref[...])
    off = HALO - CONV_WIDTH // 2
    span = ush.shape[1]
    for b in range(SUBLANES):
        ush[b] = ubuf[off + b:off + b + span, :]

    conv = jnp.zeros((tm, CONV_DIM), F32) + dwb_ref[...]
    for b in range(SUBLANES):
        for w in range(b, CONV_WIDTH, SUBLANES):
            conv = conv + ush[b, w - b:w - b + tm, :] * dwk_ref[w:w + 1, :]
    c = _ln(conv, clg_ref[...], clb_ref[...])
    c = c * jax.nn.sigmoid(c)
    conv_out = _dot(c.astype(BF16), wco_ref[...])

    x = x_ref[...]
    gates = jax.nn.sigmoid(_dot(x.astype(BF16), wg_ref[...]) + bg_ref[...])
    mla_out = _dot(o_ref[...], wmo_ref[...])
    merged = gates[:, :D_MODEL] * conv_out + gates[:, D_MODEL:] * mla_out
    m = _dot(merged.astype(BF16), wo_ref[...])
    x1 = _ln(DEEPNORM_ALPHA * x + m, l1g_ref[...], l1b_ref[...])
    x1_ref[...] = x1

    hi = x1.astype(BF16)
    lo = (x1 - hi.astype(F32)).astype(BF16)
    logits = _dot(hi, wrh_ref[...]) + _dot(lo, wrh_ref[...]) + _dot(hi, wrl_ref[...]) + br_ref[...]
    g_idx, i1, i2, _, _, lane = _route(logits)
    la = jnp.minimum(i1, i2) - g_idx * EXPERTS_PER_GROUP
    lb = jnp.maximum(i1, i2) - g_idx * EXPERTS_PER_GROUP
    pair = jnp.right_shift(la * (2 * EXPERTS_PER_GROUP - 1 - la), 1) + (lb - la - 1)
    cls = g_idx * PAIRS_PER_GROUP + pair
    onehot = lane == cls

    @pl.when(i == 0)
    def _():
        cnt_scr[...] = jnp.zeros_like(cnt_scr)

    r_id = lax.broadcasted_iota(jnp.int32, (tm, tm), 0)
    c_id = lax.broadcasted_iota(jnp.int32, (tm, tm), 1)
    earlier = _dot((c_id < r_id).astype(BF16), onehot.astype(BF16))
    rank = jnp.sum(jnp.where(onehot, earlier + cnt_scr[...], 0.0), axis=1, keepdims=True)
    cnt_scr[...] += jnp.sum(onehot.astype(F32), axis=0, keepdims=True)
    cnt_ref[...] = cnt_scr[...]
    route_ref[...] = jnp.where(lane == 0, cls.astype(F32), jnp.where(lane == 1, rank, 0.0))


def _post_call(x, u, o, lw, S, tm):
    T = x.shape[0]
    nper = S // tm
    nh = tm // HALO
    last_halo = T // HALO - 1
    row = lambda i: (i, 0)
    const = lambda i: (0, 0)
    full = lambda a: pl.BlockSpec(a.shape, const)
    names = ['dwk', 'dwb', 'clg', 'clb', 'wco', 'wg', 'bg', 'wmo', 'wo', 'l1g', 'l1b', 'wrh', 'wrl', 'br']
    ws = [lw[n] for n in names]
    return pl.pallas_call(
        functools.partial(_post_kernel, nper=nper),
        grid=(T // tm,),
        in_specs=[pl.BlockSpec((tm, D_MODEL), row), pl.BlockSpec((tm, CONV_DIM), row),
                  pl.BlockSpec((HALO, CONV_DIM), lambda i: (jnp.maximum(i * nh - 1, 0), 0)),
                  pl.BlockSpec((HALO, CONV_DIM), lambda i: (jnp.minimum((i + 1) * nh, last_halo), 0)),
                  pl.BlockSpec((tm, N_HEADS * V_HEAD), row)] + [full(a) for a in ws],
        out_specs=[pl.BlockSpec((tm, D_MODEL), row), pl.BlockSpec((tm, LANES), row),
                   pl.BlockSpec((1, LANES), const)],
        out_shape=[jax.ShapeDtypeStruct((T, D_MODEL), F32), jax.ShapeDtypeStruct((T, LANES), F32),
                   jax.ShapeDtypeStruct((1, LANES), F32)],
        scratch_shapes=[pltpu.VMEM((tm + 2 * HALO, CONV_DIM), F32),
                        pltpu.VMEM((SUBLANES, tm + (CONV_WIDTH - 1) // SUBLANES * SUBLANES, CONV_DIM), F32),
                        pltpu.VMEM((1, LANES), F32)],
        compiler_params=pltpu.CompilerParams(dimension_semantics=("arbitrary",), vmem_limit_bytes=VMEM_LIMIT),
    )(x, u, u, u, o, *ws)


def _class_tables(route, cnt, tile, n_tiles):
    cls = route[:, 0].astype(jnp.int32)
    rank = route[:, 1].astype(jnp.int32)
    counts = cnt[0, :N_CLASSES].astype(jnp.int32)
    tiles_of = (counts + tile - 1) // tile
    t_end = jnp.cumsum(tiles_of)
    t_start = t_end - tiles_of
    class_ids = jnp.arange(N_CLASSES, dtype=jnp.int32)[None, :]

    def lookup(idx, table):
        return jnp.sum(jnp.where(idx[:, None] == class_ids, table[None, :], 0), axis=1)

    pos = lookup(cls, t_start * tile) + rank
    g = jnp.arange(n_tiles, dtype=jnp.int32)
    blk = jnp.minimum(g, t_end[-1] - 1)
    c = jnp.sum((blk[:, None] >= t_end[None, :]).astype(jnp.int32), axis=1)
    first_row = (blk - lookup(c, t_start)) * tile
    valid = jnp.where(g < t_end[-1], jnp.clip(lookup(c, counts) - first_row, 0, tile), 0)
    pairs = [(a, b) for a in range(EXPERTS_PER_GROUP) for b in range(a + 1, EXPERTS_PER_GROUP)]
    group_lo = (class_ids[0] // PAIRS_PER_GROUP) * EXPERTS_PER_GROUP
    ea = lookup(c, group_lo + jnp.asarray([a for a, _ in pairs] * N_GROUPS, jnp.int32))
    eb = lookup(c, group_lo + jnp.asarray([b for _, b in pairs] * N_GROUPS, jnp.int32))
    return pos, blk, ea, eb, valid.astype(jnp.int32)


def _sorted_row(ref3, pos):
    return ref3.at[lax.shift_right_logical(pos, 3), pl.ds(lax.bitwise_and(pos, SUBLANES - 1), 1)]


def _scatter_kernel(pos_ref, x_ref, xs_ref, sem):
    groups = x_ref.shape[0]

    def start(i, carry):
        for u in range(SUBLANES):
            src = x_ref.at[i, pl.ds(u, 1)]
            pltpu.make_async_copy(src, _sorted_row(xs_ref, pos_ref[0, 0, i * SUBLANES + u]), sem).start()
        return carry

    lax.fori_loop(0, groups, start, 0)
    pltpu.make_async_copy(x_ref, xs_ref.at[pl.ds(0, groups)], sem).wait()


def _gather_kernel(pos_ref, xs_ref, o_ref, sem):
    groups = o_ref.shape[0]

    def start(i, carry):
        for u in range(SUBLANES):
            dst = o_ref.at[i, pl.ds(u, 1)]
            pltpu.make_async_copy(_sorted_row(xs_ref, pos_ref[0, 0, i * SUBLANES + u]), dst, sem).start()
        return carry

    lax.fori_loop(0, groups, start, 0)
    pltpu.make_async_copy(xs_ref.at[pl.ds(0, groups)], o_ref, sem).wait()


def _permute_call(kernel_fn, pos, x, n_out, tm, to_sorted):
    T = pos.shape[0]
    pos3 = pos.reshape(T // tm, 1, tm)
    pos_spec = pl.BlockSpec((1, 1, tm), lambda i: (i, 0, 0), memory_space=pltpu.SMEM)
    tile_spec = pl.BlockSpec((tm // SUBLANES, SUBLANES, D_MODEL), lambda i: (i, 0, 0))
    any_spec = pl.BlockSpec(memory_space=pl.ANY)
    out = pl.pallas_call(
        kernel_fn,
        grid=(T // tm,),
        in_specs=[pos_spec, tile_spec if to_sorted else any_spec],
        out_specs=any_spec if to_sorted else tile_spec,
        out_shape=jax.ShapeDtypeStruct((n_out // SUBLANES, SUBLANES, D_MODEL), F32),
        scratch_shapes=[pltpu.SemaphoreType.DMA],
        compiler_params=pltpu.CompilerParams(dimension_semantics=("arbitrary",), vmem_limit_bytes=VMEM_LIMIT),
    )(pos3, x.reshape(-1, SUBLANES, D_MODEL))
    return out.reshape(n_out, D_MODEL)


def _moe_kernel(blk_ref, ea_ref, eb_ref, valid_ref, xs_ref, gua_ref, da_ref, gub_ref, db_ref,
                wr_ref, br_ref, l2g_ref, l2b_ref, o_ref):
    g = pl.program_id(0)
    tile = xs_ref.shape[0]
    n_valid = valid_ref[g]

    @pl.when(n_valid > 0)
    def _():
        row = lax.broadcasted_iota(jnp.int32, (tile, 1), 0)
        x = jnp.where(row < n_valid, xs_ref[...], 0.0)
        xb = x.astype(BF16)
        logits = _dot(xb, wr_ref[...]) + br_ref[...]
        lane = lax.broadcasted_iota(jnp.int32, logits.shape, 1)
        gl = jnp.where((lane >= N_EXPERTS) & (lane < N_EXPERTS + N_GROUPS), logits, -jnp.inf)
        g_w = 1.0 / jnp.sum(jnp.exp(gl - jnp.max(gl, axis=1, keepdims=True)), axis=1, keepdims=True)
        l_a = jnp.sum(jnp.where(lane == ea_ref[g], logits, 0.0), axis=1, keepdims=True)
        l_b = jnp.sum(jnp.where(lane == eb_ref[g], logits, 0.0), axis=1, keepdims=True)
        w_a = g_w / (1.0 + jnp.exp(l_b - l_a))
        w_b = g_w / (1.0 + jnp.exp(l_a - l_b))

        def expert(gu_ref, d_ref):
            gu = _dot(xb, gu_ref[0])
            gate, up = gu[:, :D_EXPERT], gu[:, D_EXPERT:]
            return _dot((gate * jax.nn.sigmoid(gate) * up).astype(BF16), d_ref[0])

        y = w_a * expert(gua_ref, da_ref) + w_b * expert(gub_ref, db_ref)
        o_ref[...] = _ln(DEEPNORM_ALPHA * x + y, l2g_ref[...], l2b_ref[...])


def _moe_call(xs, tables, lw, tile):
    _, blk, ea, eb, valid = tables
    n_tiles = blk.shape[0]
    rows = lambda g, blk, ea, eb, valid: (blk[g], 0)
    const = lambda g, blk, ea, eb, valid: (0, 0)
    of_a = lambda g, blk, ea, eb, valid: (ea[g], 0, 0)
    of_b = lambda g, blk, ea, eb, valid: (eb[g], 0, 0)
    gu_shape, d_shape = (1, D_MODEL, 2 * D_EXPERT), (1, D_EXPERT, D_MODEL)
    grid_spec = pltpu.PrefetchScalarGridSpec(
        num_scalar_prefetch=4,
        grid=(n_tiles,),
        in_specs=[pl.BlockSpec((tile, D_MODEL), rows),
                  pl.BlockSpec(gu_shape, of_a), pl.BlockSpec(d_shape, of_a),
                  pl.BlockSpec(gu_shape, of_b), pl.BlockSpec(d_shape, of_b),
                  pl.BlockSpec((D_MODEL, LANES), const), pl.BlockSpec((1, LANES), const),
                  pl.BlockSpec((1, D_MODEL), const), pl.BlockSpec((1, D_MODEL), const)],
        out_specs=pl.BlockSpec((tile, D_MODEL), rows),
    )
    return pl.pallas_call(
        _moe_kernel,
        grid_spec=grid_spec,
        out_shape=jax.ShapeDtypeStruct(xs.shape, F32),
        compiler_params=pltpu.CompilerParams(dimension_semantics=("arbitrary",), vmem_limit_bytes=VMEM_LIMIT),
    )(blk, ea, eb, valid, xs, lw['wegu'], lw['wed'], lw['wegu'], lw['wed'], lw['wrh'], lw['br'],
      lw['l2g'], lw['l2b'])


def _moe(x1, route, cnt, lw, tile, tm, row_splits):
    T = x1.shape[0]
    n_tiles = T // tile + N_CLASSES
    tables = _class_tables(route, cnt, tile, n_tiles)
    pos = tables[0]
    xs = _permute_call(_scatter_kernel, pos, x1, n_tiles * tile, tm, True)
    ys = _moe_call(xs, tables, lw, tile)
    outs, lo = [], 0
    for n in row_splits:
        outs.append(_permute_call(_gather_kernel, pos[lo:lo + n], ys, n, tm, False))
        lo += n
    return outs


def _rope_tables(S):
    inv_freq = ROPE_THETA ** (-jnp.arange(0, QK_ROPE, 2, dtype=F32) / QK_ROPE)
    ang = jnp.arange(S, dtype=F32)[:, None] * inv_freq[None, :]
    cos, sin = jnp.cos(ang), jnp.sin(ang)
    z = lambda n: jnp.zeros((S, n), F32)
    tail = HEAD_PAD - QK_NOPE - QK_ROPE
    c = jnp.concatenate([jnp.ones((S, QK_NOPE), F32), cos, cos, z(tail)], axis=1)
    s_fwd = jnp.concatenate([z(QK_NOPE + HALF_ROPE), sin, z(tail)], axis=1)
    s_bwd = jnp.concatenate([z(QK_NOPE), -sin, z(HALF_ROPE + tail)], axis=1)
    qs = SOFTMAX_SCALE * math.log2(math.e)
    return ((c * qs).T, (s_fwd * qs).T, (s_bwd * qs).T, c, s_fwd, s_bwd)


def _layer_weights(l, p):
    w_in, b_in = p['w_in'][l], p['b_in'][l]
    rope_lo = GLU_COLS + Q_LORA + KV_LORA
    gate_lo = rope_lo + QK_ROPE
    tail = HEAD_PAD - QK_NOPE - QK_ROPE
    pad_rope = lambda a: jnp.pad(a, ((0, 0), (QK_NOPE, tail)))
    w1 = jnp.concatenate([w_in[:, :rope_lo], pad_rope(w_in[:, rope_lo:gate_lo])], axis=1)
    b1 = jnp.concatenate([b_in[None, :rope_lo], pad_rope(b_in[None, rope_lo:gate_lo])], axis=1)
    wq = jnp.pad(p['w_q_b'][l].reshape(Q_LORA, N_HEADS, QK_NOPE + QK_ROPE), ((0, 0), (0, 0), (0, tail)))
    wkv = p['w_kv_b'][l].reshape(KV_LORA, N_HEADS, QK_NOPE + V_HEAD)
    wk = jnp.pad(wkv[:, :, :QK_NOPE], ((0, 0), (0, 0), (0, HEAD_PAD - QK_NOPE)))
    wv = jnp.pad(wkv[:, :, QK_NOPE:], ((0, 0), (0, 0), (0, V_PAD - V_HEAD)))
    vone = jnp.tile(jnp.arange(V_PAD) == V_HEAD, N_HEADS).astype(F32)[:, None]
    w_r = jnp.pad(jnp.concatenate([p['w_router_expert'][l], p['w_router_group'][l]], axis=1),
                  ((0, 0), (0, LANES - N_EXPERTS - N_GROUPS)))
    b_r = jnp.pad(jnp.concatenate([p['b_router_expert'][l], p['b_router_group'][l]])[None, :],
                  ((0, 0), (0, LANES - N_EXPERTS - N_GROUPS)))
    w_r_hi = w_r.astype(BF16)
    w_r_lo = (w_r - w_r_hi.astype(F32)).astype(BF16)
    r2 = lambda a: a[None, :]
    return dict(
        w1=w1.astype(BF16), b1=b1, qg=r2(p['q_norm_g'][l]), wqt=wq.reshape(Q_LORA, -1).T.astype(BF16),
        kvg=r2(p['kv_norm_g'][l]), wk=wk.reshape(KV_LORA, -1).astype(BF16),
        wvt=wv.reshape(KV_LORA, -1).T.astype(BF16), vone=vone,
        dwk=p['dw_kernel'][l], dwb=r2(p['dw_bias'][l]), clg=r2(p['conv_ln_g'][l]), clb=r2(p['conv_ln_b'][l]),
        wco=p['w_conv_out'][l].astype(BF16), wg=w_in[:, gate_lo:].astype(BF16), bg=b_in[None, gate_lo:],
        wmo=p['w_mla_out'][l].astype(BF16), wo=p['w_out'][l].astype(BF16),
        l1g=r2(p['ln1_g'][l]), l1b=r2(p['ln1_b'][l]), wrh=w_r_hi, wrl=w_r_lo, br=b_r,
        wegu=jnp.concatenate([p['w_gate_e'][l], p['w_up_e'][l]], axis=2).astype(BF16),
        wed=p['w_down_e'][l].astype(BF16),
        l2g=r2(p['ln2_g'][l]), l2b=r2(p['ln2_b'][l]),
    )


def _trunk(xa, xb, S, p, *, tm_proj=512, tq=1024, tk=1024, tm_post=512, moe_tile=256, tm_perm=1024):
    tabs = _rope_tables(S)
    T = xa.shape[0] + xb.shape[0]
    x = _layer_norm_call(xa, xb, p['ln_in_g'][None, :], p['ln_in_b'][None, :], tm_proj)
    for l in range(DEPTH):
        lw = _layer_weights(l, p)
        u, qt, k, vt = _proj_call(x, lw, tabs, S, tm_proj)
        o = _attn_call(qt, k, vt, S, tq, tk)
        x1, route, cnt = _post_call(x, u, o, lw, S, tm_post)
        splits = (xa.shape[0], xb.shape[0]) if l == DEPTH - 1 else (T,)
        outs = _moe(x1, route, cnt, lw, moe_tile, tm_perm, splits)
        x = outs[0]
    return outs


def kernel(x_prompt, x_sample, ln_in_g, ln_in_b, w_in, b_in, dw_kernel, dw_bias, conv_ln_g, conv_ln_b, w_conv_out, q_norm_g, w_q_b, kv_norm_g, w_kv_b, w_mla_out, w_out, ln1_g, ln1_b, w_router_group, b_router_group, w_router_expert, b_router_expert, w_gate_e, w_up_e, w_down_e, ln2_g, ln2_b):
    p = dict(ln_in_g=ln_in_g, ln_in_b=ln_in_b, w_in=w_in, b_in=b_in, dw_kernel=dw_kernel, dw_bias=dw_bias,
             conv_ln_g=conv_ln_g, conv_ln_b=conv_ln_b, w_conv_out=w_conv_out, q_norm_g=q_norm_g, w_q_b=w_q_b,
             kv_norm_g=kv_norm_g, w_kv_b=w_kv_b, w_mla_out=w_mla_out, w_out=w_out, ln1_g=ln1_g, ln1_b=ln1_b,
             w_router_group=w_router_group, b_router_group=b_router_group,
             w_router_expert=w_router_expert, b_router_expert=b_router_expert,
             w_gate_e=w_gate_e, w_up_e=w_up_e, w_down_e=w_down_e, ln2_g=ln2_g, ln2_b=ln2_b)
    S = x_prompt.shape[1]
    assert x_sample.shape[1] == S and x_prompt.shape[2] == D_MODEL
    y_prompt, y_sample = _trunk(x_prompt.reshape(-1, D_MODEL), x_sample.reshape(-1, D_MODEL), S, p)
    return (y_prompt.reshape(x_prompt.shape), y_sample.reshape(x_sample.shape))
```

```python
import functools
import math

import jax
import jax.numpy as jnp
from jax import lax
from jax.experimental import pallas as pl
from jax.experimental.pallas import tpu as pltpu

D_MODEL = 1024
DEPTH = 2
CONV_DIM = 512
CONV_WIDTH = 31
N_HEADS = 8
QK_NOPE = 64
QK_ROPE = 32
V_HEAD = 64
Q_LORA = 384
KV_LORA = 256
ROPE_THETA = 10000.0
SOFTMAX_SCALE = (QK_NOPE + QK_ROPE) ** -0.5
N_GROUPS = 4
EXPERTS_PER_GROUP = 8
N_EXPERTS = N_GROUPS * EXPERTS_PER_GROUP
D_EXPERT = 256
DEEPNORM_ALPHA = (2 * DEPTH) ** 0.25
LN_EPS = 1e-5

LANES = 128
HEAD_PAD = LANES
HALF_ROPE = QK_ROPE // 2
V_PAD = 80
HALO = 16
GLU_COLS = 2 * CONV_DIM
SUBLANES = 8
MXU_COLS = 256
MAX_ROWS = 64
PAIRS_PER_GROUP = EXPERTS_PER_GROUP * (EXPERTS_PER_GROUP - 1) // 2
N_CLASSES = N_GROUPS * PAIRS_PER_GROUP
VMEM_LIMIT = 56 * 1024 * 1024

BF16 = jnp.bfloat16
F32 = jnp.float32


def _ln(x, g, b):
    mu = jnp.mean(x, axis=-1, keepdims=True)
    xc = x - mu
    var = jnp.mean(xc * xc, axis=-1, keepdims=True)
    return xc * lax.rsqrt(var + LN_EPS) * g + b


def _rms(x, g):
    return x * lax.rsqrt(jnp.mean(x * x, axis=-1, keepdims=True) + LN_EPS) * g


def _dot(a, b):
    return jnp.dot(a, b, preferred_element_type=F32)


def _rope(x, c, s_fwd, s_bwd):
    return x * c + pltpu.roll(x, HALF_ROPE, 1) * s_fwd + pltpu.roll(x, HEAD_PAD - HALF_ROPE, 1) * s_bwd


def _rope_t(x, c, s_fwd, s_bwd):
    return x * c + pltpu.roll(x, HALF_ROPE, 0) * s_fwd + pltpu.roll(x, HEAD_PAD - HALF_ROPE, 0) * s_bwd


def _dot_nt(a, b):
    return lax.dot_general(a, b, (((1,), (1,)), ((), ())), preferred_element_type=F32)


def _proj_kernel(x_ref, *refs):
    _proj_body(x_ref[...], *refs)


def _proj_first_kernel(xa_ref, xb_ref, g_ref, b_ref, *refs, tiles_a):
    x = jnp.where(pl.program_id(0) < tiles_a, xa_ref[...], xb_ref[...])
    x = _ln(x, g_ref[...], b_ref[...])
    refs[-1][...] = x
    _proj_body(x, *refs[:-1])


def _proj_body(x, w1_ref, b1_ref, qg_ref, wqt_ref, kvg_ref, wk_ref, wvt_ref, vone_ref,
               cq_ref, sfq_ref, sbq_ref, ck_ref, sfk_ref, sbk_ref,
               u_ref, qt_ref, k_ref, vt_ref):
    h = _dot(x.astype(BF16), w1_ref[...]) + b1_ref[...]
    a = h[:, :CONV_DIM]
    g = h[:, CONV_DIM:GLU_COLS]
    u_ref[...] = a * jax.nn.sigmoid(g)
    qa = h[:, GLU_COLS:GLU_COLS + Q_LORA]
    kva = h[:, GLU_COLS + Q_LORA:GLU_COLS + Q_LORA + KV_LORA]
    kr = h[:, GLU_COLS + Q_LORA + KV_LORA:]

    qt = _dot_nt(wqt_ref[...], _rms(qa, qg_ref[...]).astype(BF16))
    cq, sfq, sbq = cq_ref[...], sfq_ref[...], sbq_ref[...]
    for hd in range(N_HEADS):
        sl = slice(hd * HEAD_PAD, (hd + 1) * HEAD_PAD)
        qt_ref[sl, :] = _rope_t(qt[sl, :], cq, sfq, sbq).astype(BF16)

    kvn = _rms(kva, kvg_ref[...]).astype(BF16)
    kn = _dot(kvn, wk_ref[...])
    krr = _rope(kr, ck_ref[...], sfk_ref[...], sbk_ref[...])
    for hd in range(N_HEADS):
        sl = slice(hd * HEAD_PAD, (hd + 1) * HEAD_PAD)
        k_ref[:, sl] = (kn[:, sl] + krr).astype(BF16)
    vt_ref[...] = (_dot_nt(wvt_ref[...], kvn) + vone_ref[...]).astype(BF16)


def _proj_call(xs, lw, tabs, S, tm, ln_in=None):
    T = sum(x.shape[0] for x in xs)
    nper = S // tm
    row = lambda i: (i, 0)
    col = lambda i: (0, i)
    const = lambda i: (0, 0)
    full = lambda a: pl.BlockSpec(a.shape, const)
    tab_spec = pl.BlockSpec((tm, HEAD_PAD), lambda i: (i % nper, 0))
    tabt_spec = pl.BlockSpec((HEAD_PAD, tm), lambda i: (0, i % nper))
    wide = N_HEADS * HEAD_PAD
    vrows = N_HEADS * V_PAD
    weights = [lw[n] for n in ('w1', 'b1', 'qg', 'wqt', 'kvg', 'wk', 'wvt', 'vone')]
    in_specs = [full(a) for a in weights] + [tabt_spec] * 3 + [tab_spec] * 3
    out_specs = [pl.BlockSpec((tm, CONV_DIM), row), pl.BlockSpec((wide, tm), col),
                 pl.BlockSpec((tm, wide), row), pl.BlockSpec((vrows, tm), col)]
    out_shape = [jax.ShapeDtypeStruct((T, CONV_DIM), F32), jax.ShapeDtypeStruct((wide, T), BF16),
                 jax.ShapeDtypeStruct((T, wide), BF16), jax.ShapeDtypeStruct((vrows, T), BF16)]
    if ln_in is None:
        body = _proj_kernel
        in_specs = [pl.BlockSpec((tm, D_MODEL), row)] + in_specs
        args = list(xs) + weights + list(tabs)
    else:
        tiles_a = xs[0].shape[0] // tm
        body = functools.partial(_proj_first_kernel, tiles_a=tiles_a)
        in_specs = [pl.BlockSpec((tm, D_MODEL), lambda i: (jnp.minimum(i, tiles_a - 1), 0)),
                    pl.BlockSpec((tm, D_MODEL), lambda i: (jnp.maximum(i - tiles_a, 0), 0)),
                    pl.BlockSpec((1, D_MODEL), const), pl.BlockSpec((1, D_MODEL), const)] + in_specs
        out_specs = out_specs + [pl.BlockSpec((tm, D_MODEL), row)]
        out_shape = out_shape + [jax.ShapeDtypeStruct((T, D_MODEL), F32)]
        args = list(xs) + list(ln_in) + weights + list(tabs)
    return pl.pallas_call(
        body,
        grid=(T // tm,),
        in_specs=in_specs,
        out_specs=out_specs,
        out_shape=out_shape,
        compiler_params=pltpu.CompilerParams(dimension_semantics=("parallel",), vmem_limit_bytes=VMEM_LIMIT),
    )(*args)


def _attn_kernel(qt_ref, k_ref, vt_ref, o_ref, *scratch, tk):
    s_scr = (scratch[0:2], scratch[2:4])
    p_scr = (scratch[4:6], scratch[6:8])
    mx_scr, a_scr = scratch[8:10], scratch[10:12]
    m_scr, acc_scr = scratch[12:]
    n = k_ref.shape[0] // tk

    def key_rows(j):
        return pl.ds(pl.multiple_of(j * tk, tk), tk)

    pieces = [(hh, slice(None)) for hh in range(2)]

    def scores(j, slot, hh, cols):
        sl = slice(hh * HEAD_PAD, (hh + 1) * HEAD_PAD)
        s = _dot(k_ref[key_rows(j), sl], qt_ref[sl, cols])
        s_scr[slot][hh][:, cols] = s
        part = jnp.max(s.reshape(tk // MAX_ROWS, MAX_ROWS, s.shape[1]), axis=0)
        mx_scr[slot][hh, :, cols] = jnp.max(part, axis=0, keepdims=True)

    def softmax(slot, hh, cols):
        m_old = m_scr[hh, :, cols]
        m_new = jnp.maximum(m_old, mx_scr[slot][hh, :, cols])
        p_scr[slot][hh][:, cols] = jnp.exp2(s_scr[slot][hh][:, cols] - m_new).astype(BF16)
        a_scr[slot][hh, :, cols] = jnp.exp2(m_old - m_new)
        m_scr[hh, :, cols] = m_new

    def accum(j, slot, hh, cols):
        vt = vt_ref[hh * V_PAD:(hh + 1) * V_PAD, key_rows(j)]
        acc_scr[hh, :, cols] = (a_scr[slot][hh, :, cols] * acc_scr[hh, :, cols]
                                + _dot(vt, p_scr[slot][hh][:, cols]))

    def stages(score_args=None, softmax_slot=None, accum_args=None):
        for piece in pieces if score_args is not None else ():
            scores(*score_args, *piece)
        for piece in pieces if softmax_slot is not None else ():
            softmax(softmax_slot, *piece)
        for piece in pieces if accum_args is not None else ():
            accum(*accum_args, *piece)

    m_scr[...] = jnp.full(m_scr.shape, -jnp.inf, F32)
    acc_scr[...] = jnp.zeros(acc_scr.shape, F32)
    stages(score_args=(0, 0))
    stages(score_args=(1, 1), softmax_slot=0)

    def body(t, carry):
        j = 2 * t
        stages((j + 2, 0), 1, (j, 0))
        stages((j + 3, 1), 0, (j + 1, 1))
        return carry

    lax.fori_loop(0, n // 2 - 1, body, 0)
    stages(softmax_slot=1, accum_args=(n - 2, 0))
    stages(accum_args=(n - 1, 1))
    ot = jnp.concatenate([acc_scr[hh, :V_HEAD, :] / acc_scr[hh, V_HEAD:V_HEAD + 1, :] for hh in range(2)], axis=0)
    o_ref[...] = ot.T.astype(BF16)


def _attn_call(qt, k, vt, S, tq, tk):
    T = k.shape[0]
    nb = T // S
    nq = S // tq
    assert (S // tk) % 2 == 0
    return pl.pallas_call(
        functools.partial(_attn_kernel, tk=tk),
        grid=(nb, N_HEADS // 2, nq),
        in_specs=[pl.BlockSpec((2 * HEAD_PAD, tq), lambda b, hp, i: (hp, b * nq + i)),
                  pl.BlockSpec((S, 2 * HEAD_PAD), lambda b, hp, i: (b, hp)),
                  pl.BlockSpec((2 * V_PAD, S), lambda b, hp, i: (hp, b))],
        out_specs=pl.BlockSpec((tq, 2 * V_HEAD), lambda b, hp, i: (b * nq + i, hp)),
        out_shape=jax.ShapeDtypeStruct((T, N_HEADS * V_HEAD), BF16),
        scratch_shapes=[pltpu.VMEM((tk, tq), F32)] * 4 + [pltpu.VMEM((tk, tq), BF16)] * 4 + [
                        pltpu.VMEM((2, 1, tq), F32)] * 4 + [
                        pltpu.VMEM((2, 1, tq), F32),
                        pltpu.VMEM((2, V_PAD, tq), F32)],
        compiler_params=pltpu.CompilerParams(dimension_semantics=("parallel", "parallel", "parallel"),
                                             vmem_limit_bytes=VMEM_LIMIT),
    )(qt, k, vt)


def _route(logits):
    lane = lax.broadcasted_iota(jnp.int32, logits.shape, 1)
    neg = -jnp.inf
    is_group = (lane >= N_EXPERTS) & (lane < N_EXPERTS + N_GROUPS)
    gl = jnp.where(is_group, logits, neg)
    gmax = jnp.max(gl, axis=1, keepdims=True)
    g_w = 1.0 / jnp.sum(jnp.exp(gl - gmax), axis=1, keepdims=True)
    g_idx = jnp.min(jnp.where(gl == gmax, lane, 2 * LANES), axis=1, keepdims=True) - N_EXPERTS
    lo = g_idx * EXPERTS_PER_GROUP
    el = jnp.where((lane >= lo) & (lane < lo + EXPERTS_PER_GROUP), logits, neg)
    m1 = jnp.max(el, axis=1, keepdims=True)
    i1 = jnp.min(jnp.where(el == m1, lane, 2 * LANES), axis=1, keepdims=True)
    el2 = jnp.where(lane == i1, neg, el)
    m2 = jnp.max(el2, axis=1, keepdims=True)
    i2 = jnp.min(jnp.where(el2 == m2, lane, 2 * LANES), axis=1, keepdims=True)
    r = jnp.exp(m2 - m1)
    w1 = g_w / (1.0 + r)
    w2 = g_w * r / (1.0 + r)
    return g_idx, i1, i2, w1, w2, lane


def _post_kernel(x_ref, u_ref, up_ref, un_ref, o_ref,
                 dwk_ref, dwb_ref, clg_ref, clb_ref, wco_ref, wg_ref, bg_ref, wmo_ref, wo_ref,
                 l1g_ref, l1b_ref, wrh_ref, wrl_ref, br_ref,
                 x1_ref, route_ref, cnt_ref, ubuf, ush, cnt_scr, *, nper):
    tm = x_ref.shape[0]
    i = pl.program_id(0)
    first = (i % nper) == 0
    last = (i % nper) == nper - 1
    ubuf[0:HALO, :] = jnp.where(first, 0.0, up_ref[...])
    ubuf[HALO:HALO + tm, :] = u_ref[...]
    ubuf[HALO + tm:, :] = jnp.where(last, 0.0, un_ref[...])
    off = HALO - CONV_WIDTH // 2
    span = ush.shape[1]
    for b in range(SUBLANES):
        ush[b] = ubuf[off + b:off + b + span, :]

    conv = jnp.zeros((tm, CONV_DIM), F32) + dwb_ref[...]
    for b in range(SUBLANES):
        for w in range(b, CONV_WIDTH, SUBLANES):
            conv = conv + ush[b, w - b:w - b + tm, :] * dwk_ref[w:w + 1, :]
    c = _ln(conv, clg_ref[...], clb_ref[...])
    c = c * jax.nn.sigmoid(c)
    conv_out = _dot(c.astype(BF16), wco_ref[...])

    x = x_ref[...]
    gates = jax.nn.sigmoid(_dot(x.astype(BF16), wg_ref[...]) + bg_ref[...])
    mla_out = _dot(o_ref[...], wmo_ref[...])
    merged = gates[:, :D_MODEL] * conv_out + gates[:, D_MODEL:] * mla_out
    m = _dot(merged.astype(BF16), wo_ref[...])
    x1 = _ln(DEEPNORM_ALPHA * x + m, l1g_ref[...], l1b_ref[...])
    x1_ref[...] = x1

    hi = x1.astype(BF16)
    lo = (x1 - hi.astype(F32)).astype(BF16)
    logits = _dot(hi, wrh_ref[...]) + _dot(lo, wrh_ref[...]) + _dot(hi, wrl_ref[...]) + br_ref[...]
    g_idx, i1, i2, _, _, lane = _route(logits)
    la = jnp.minimum(i1, i2) - g_idx * EXPERTS_PER_GROUP
    lb = jnp.maximum(i1, i2) - g_idx * EXPERTS_PER_GROUP
    pair = jnp.right_shift(la * (2 * EXPERTS_PER_GROUP - 1 - la), 1) + (lb - la - 1)
    cls = g_idx * PAIRS_PER_GROUP + pair
    onehot = lane == cls

    @pl.when(i == 0)
    def _():
        cnt_scr[...] = jnp.zeros_like(cnt_scr)

    r_id = lax.broadcasted_iota(jnp.int32, (tm, tm), 0)
    c_id = lax.broadcasted_iota(jnp.int32, (tm, tm), 1)
    earlier = _dot((c_id < r_id).astype(BF16), onehot.astype(BF16))
    rank = jnp.sum(jnp.where(onehot, earlier + cnt_scr[...], 0.0), axis=1, keepdims=True)
    cnt_scr[...] += jnp.sum(onehot.astype(F32), axis=0, keepdims=True)
    cnt_ref[...] = cnt_scr[...]
    route_ref[...] = jnp.where(lane == 0, cls.astype(F32), jnp.where(lane == 1, rank, 0.0))


def _post_call(x, u, o, lw, S, tm):
    T = x.shape[0]
    nper = S // tm
    nh = tm // HALO
    last_halo = T // HALO - 1
    row = lambda i: (i, 0)
    const = lambda i: (0, 0)
    full = lambda a: pl.BlockSpec(a.shape, const)
    names = ['dwk', 'dwb', 'clg', 'clb', 'wco', 'wg', 'bg', 'wmo', 'wo', 'l1g', 'l1b', 'wrh', 'wrl', 'br']
    ws = [lw[n] for n in names]
    return pl.pallas_call(
        functools.partial(_post_kernel, nper=nper),
        grid=(T // tm,),
        in_specs=[pl.BlockSpec((tm, D_MODEL), row), pl.BlockSpec((tm, CONV_DIM), row),
                  pl.BlockSpec((HALO, CONV_DIM), lambda i: (jnp.maximum(i * nh - 1, 0), 0)),
                  pl.BlockSpec((HALO, CONV_DIM), lambda i: (jnp.minimum((i + 1) * nh, last_halo), 0)),
                  pl.BlockSpec((tm, N_HEADS * V_HEAD), row)] + [full(a) for a in ws],
        out_specs=[pl.BlockSpec((tm, D_MODEL), row), pl.BlockSpec((tm, LANES), row),
                   pl.BlockSpec((1, LANES), const)],
        out_shape=[jax.ShapeDtypeStruct((T, D_MODEL), F32), jax.ShapeDtypeStruct((T, LANES), F32),
                   jax.ShapeDtypeStruct((1, LANES), F32)],
        scratch_shapes=[pltpu.VMEM((tm + 2 * HALO, CONV_DIM), F32),
                        pltpu.VMEM((SUBLANES, tm + (CONV_WIDTH - 1) // SUBLANES * SUBLANES, CONV_DIM), F32),
                        pltpu.VMEM((1, LANES), F32)],
        compiler_params=pltpu.CompilerParams(dimension_semantics=("arbitrary",), vmem_limit_bytes=VMEM_LIMIT),
    )(x, u, u, u, o, *ws)


def _class_tables(route, cnt, tile, n_tiles):
    cls = route[:, 0].astype(jnp.int32)
    rank = route[:, 1].astype(jnp.int32)
    counts = cnt[0, :N_CLASSES].astype(jnp.int32)
    tiles_of = (counts + tile - 1) // tile
    t_end = jnp.cumsum(tiles_of)
    t_start = t_end - tiles_of
    class_ids = jnp.arange(N_CLASSES, dtype=jnp.int32)[None, :]

    def lookup(idx, table):
        return jnp.sum(jnp.where(idx[:, None] == class_ids, table[None, :], 0), axis=1)

    pos = lookup(cls, t_start * tile) + rank
    g = jnp.arange(n_tiles, dtype=jnp.int32)
    blk = jnp.minimum(g, t_end[-1] - 1)
    c = jnp.sum((blk[:, None] >= t_end[None, :]).astype(jnp.int32), axis=1)
    first_row = (blk - lookup(c, t_start)) * tile
    valid = jnp.where(g < t_end[-1], jnp.clip(lookup(c, counts) - first_row, 0, tile), 0)
    pairs = [(a, b) for a in range(EXPERTS_PER_GROUP) for b in range(a + 1, EXPERTS_PER_GROUP)]
    group_lo = (class_ids[0] // PAIRS_PER_GROUP) * EXPERTS_PER_GROUP
    ea = lookup(c, group_lo + jnp.asarray([a for a, _ in pairs] * N_GROUPS, jnp.int32))
    eb = lookup(c, group_lo + jnp.asarray([b for _, b in pairs] * N_GROUPS, jnp.int32))
    return pos, blk, ea, eb, valid.astype(jnp.int32)


def _sorted_row(ref3, pos):
    return ref3.at[lax.shift_right_logical(pos, 3), pl.ds(lax.bitwise_and(pos, SUBLANES - 1), 1)]


def _scatter_kernel(pos_ref, x_ref, xs_ref, sem):
    groups = x_ref.shape[0]

    def start(i, carry):
        for u in range(SUBLANES):
            src = x_ref.at[i, pl.ds(u, 1)]
            pltpu.make_async_copy(src, _sorted_row(xs_ref, pos_ref[0, 0, i * SUBLANES + u]), sem).start()
        return carry

    lax.fori_loop(0, groups, start, 0)
    pltpu.make_async_copy(x_ref, xs_ref.at[pl.ds(0, groups)], sem).wait()


def _gather_kernel(pos_ref, xs_ref, o_ref, sem):
    groups = o_ref.shape[0]

    def start(i, carry):
        for u in range(SUBLANES):
            dst = o_ref.at[i, pl.ds(u, 1)]
            pltpu.make_async_copy(_sorted_row(xs_ref, pos_ref[0, 0, i * SUBLANES + u]), dst, sem).start()
        return carry

    lax.fori_loop(0, groups, start, 0)
    pltpu.make_async_copy(xs_ref.at[pl.ds(0, groups)], o_ref, sem).wait()


def _permute_call(kernel_fn, pos, x, n_out, tm, to_sorted):
    T = pos.shape[0]
    pos3 = pos.reshape(T // tm, 1, tm)
    pos_spec = pl.BlockSpec((1, 1, tm), lambda i: (i, 0, 0), memory_space=pltpu.SMEM)
    tile_spec = pl.BlockSpec((tm // SUBLANES, SUBLANES, D_MODEL), lambda i: (i, 0, 0))
    any_spec = pl.BlockSpec(memory_space=pl.ANY)
    out = pl.pallas_call(
        kernel_fn,
        grid=(T // tm,),
        in_specs=[pos_spec, tile_spec if to_sorted else any_spec],
        out_specs=any_spec if to_sorted else tile_spec,
        out_shape=jax.ShapeDtypeStruct((n_out // SUBLANES, SUBLANES, D_MODEL), F32),
        scratch_shapes=[pltpu.SemaphoreType.DMA],
        compiler_params=pltpu.CompilerParams(dimension_semantics=("arbitrary",), vmem_limit_bytes=VMEM_LIMIT),
    )(pos3, x.reshape(-1, SUBLANES, D_MODEL))
    return out.reshape(n_out, D_MODEL)


def _moe_kernel(blk_ref, ea_ref, eb_ref, valid_ref, xs_ref, gua_ref, da_ref, gub_ref, db_ref,
                wr_ref, br_ref, l2g_ref, l2b_ref, o_ref):
    g = pl.program_id(0)
    tile = xs_ref.shape[0]
    n_valid = valid_ref[g]

    @pl.when(n_valid > 0)
    def _():
        row = lax.broadcasted_iota(jnp.int32, (tile, 1), 0)
        x = jnp.where(row < n_valid, xs_ref[...], 0.0)
        xb = x.astype(BF16)
        logits = _dot(xb, wr_ref[...]) + br_ref[...]
        lane = lax.broadcasted_iota(jnp.int32, logits.shape, 1)
        gl = jnp.where((lane >= N_EXPERTS) & (lane < N_EXPERTS + N_GROUPS), logits, -jnp.inf)
        g_w = 1.0 / jnp.sum(jnp.exp(gl - jnp.max(gl, axis=1, keepdims=True)), axis=1, keepdims=True)
        l_a = jnp.sum(jnp.where(lane == ea_ref[g], logits, 0.0), axis=1, keepdims=True)
        l_b = jnp.sum(jnp.where(lane == eb_ref[g], logits, 0.0), axis=1, keepdims=True)
        w_a = g_w / (1.0 + jnp.exp(l_b - l_a))
        w_b = g_w / (1.0 + jnp.exp(l_a - l_b))

        def expert(gu_ref, d_ref):
            gu = _dot(xb, gu_ref[0])
            gate, up = gu[:, :D_EXPERT], gu[:, D_EXPERT:]
            return _dot((gate * jax.nn.sigmoid(gate) * up).astype(BF16), d_ref[0])

        y = w_a * expert(gua_ref, da_ref) + w_b * expert(gub_ref, db_ref)
        o_ref[...] = _ln(DEEPNORM_ALPHA * x + y, l2g_ref[...], l2b_ref[...])


def _moe_call(xs, tables, lw, tile):
    _, blk, ea, eb, valid = tables
    n_tiles = blk.shape[0]
    rows = lambda g, blk, ea, eb, valid: (blk[g], 0)
    const = lambda g, blk, ea, eb, valid: (0, 0)
    of_a = lambda g, blk, ea, eb, valid: (ea[g], 0, 0)
    of_b = lambda g, blk, ea, eb, valid: (eb[g], 0, 0)
    gu_shape, d_shape = (1, D_MODEL, 2 * D_EXPERT), (1, D_EXPERT, D_MODEL)
    grid_spec = pltpu.PrefetchScalarGridSpec(
        num_scalar_prefetch=4,
        grid=(n_tiles,),
        in_specs=[pl.BlockSpec((tile, D_MODEL), rows),
                  pl.BlockSpec(gu_shape, of_a), pl.BlockSpec(d_shape, of_a),
                  pl.BlockSpec(gu_shape, of_b), pl.BlockSpec(d_shape, of_b),
                  pl.BlockSpec((D_MODEL, LANES), const), pl.BlockSpec((1, LANES), const),
                  pl.BlockSpec((1, D_MODEL), const), pl.BlockSpec((1, D_MODEL), const)],
        out_specs=pl.BlockSpec((tile, D_MODEL), rows),
    )
    return pl.pallas_call(
        _moe_kernel,
        grid_spec=grid_spec,
        out_shape=jax.ShapeDtypeStruct(xs.shape, F32),
        compiler_params=pltpu.CompilerParams(dimension_semantics=("arbitrary",), vmem_limit_bytes=VMEM_LIMIT),
    )(blk, ea, eb, valid, xs, lw['wegu'], lw['wed'], lw['wegu'], lw['wed'], lw['wrh'], lw['br'],
      lw['l2g'], lw['l2b'])


def _moe(x1, route, cnt, lw, tile, tm, row_splits):
    T = x1.shape[0]
    n_tiles = T // tile + N_CLASSES
    tables = _class_tables(route, cnt, tile, n_tiles)
    pos = tables[0]
    xs = _permute_call(_scatter_kernel, pos, x1, n_tiles * tile, tm, True)
    ys = _moe_call(xs, tables, lw, tile)
    outs, lo = [], 0
    for n in row_splits:
        outs.append(_permute_call(_gather_kernel, pos[lo:lo + n], ys, n, tm, False))
        lo += n
    return outs


def _rope_tables(S):
    inv_freq = ROPE_THETA ** (-jnp.arange(0, QK_ROPE, 2, dtype=F32) / QK_ROPE)
    ang = jnp.arange(S, dtype=F32)[:, None] * inv_freq[None, :]
    cos, sin = jnp.cos(ang), jnp.sin(ang)
    z = lambda n: jnp.zeros((S, n), F32)
    tail = HEAD_PAD - QK_NOPE - QK_ROPE
    c = jnp.concatenate([jnp.ones((S, QK_NOPE), F32), cos, cos, z(tail)], axis=1)
    s_fwd = jnp.concatenate([z(QK_NOPE + HALF_ROPE), sin, z(tail)], axis=1)
    s_bwd = jnp.concatenate([z(QK_NOPE), -sin, z(HALF_ROPE + tail)], axis=1)
    qs = SOFTMAX_SCALE * math.log2(math.e)
    return ((c * qs).T, (s_fwd * qs).T, (s_bwd * qs).T, c, s_fwd, s_bwd)


def _layer_weights(l, p):
    w_in, b_in = p['w_in'][l], p['b_in'][l]
    rope_lo = GLU_COLS + Q_LORA + KV_LORA
    gate_lo = rope_lo + QK_ROPE
    tail = HEAD_PAD - QK_NOPE - QK_ROPE
    pad_rope = lambda a: jnp.pad(a, ((0, 0), (QK_NOPE, tail)))
    w1 = jnp.concatenate([w_in[:, :rope_lo], pad_rope(w_in[:, rope_lo:gate_lo])], axis=1)
    b1 = jnp.concatenate([b_in[None, :rope_lo], pad_rope(b_in[None, rope_lo:gate_lo])], axis=1)
    wq = jnp.pad(p['w_q_b'][l].reshape(Q_LORA, N_HEADS, QK_NOPE + QK_ROPE), ((0, 0), (0, 0), (0, tail)))
    wkv = p['w_kv_b'][l].reshape(KV_LORA, N_HEADS, QK_NOPE + V_HEAD)
    wk = jnp.pad(wkv[:, :, :QK_NOPE], ((0, 0), (0, 0), (0, HEAD_PAD - QK_NOPE)))
    wv = jnp.pad(wkv[:, :, QK_NOPE:], ((0, 0), (0, 0), (0, V_PAD - V_HEAD)))
    vone = jnp.tile(jnp.arange(V_PAD) == V_HEAD, N_HEADS).astype(F32)[:, None]
    w_r = jnp.pad(jnp.concatenate([p['w_router_expert'][l], p['w_router_group'][l]], axis=1),
                  ((0, 0), (0, LANES - N_EXPERTS - N_GROUPS)))
    b_r = jnp.pad(jnp.concatenate([p['b_router_expert'][l], p['b_router_group'][l]])[None, :],
                  ((0, 0), (0, LANES - N_EXPERTS - N_GROUPS)))
    w_r_hi = w_r.astype(BF16)
    w_r_lo = (w_r - w_r_hi.astype(F32)).astype(BF16)
    r2 = lambda a: a[None, :]
    return dict(
        w1=w1.astype(BF16), b1=b1, qg=r2(p['q_norm_g'][l]), wqt=wq.reshape(Q_LORA, -1).T.astype(BF16),
        kvg=r2(p['kv_norm_g'][l]), wk=wk.reshape(KV_LORA, -1).astype(BF16),
        wvt=wv.reshape(KV_LORA, -1).T.astype(BF16), vone=vone,
        dwk=p['dw_kernel'][l], dwb=r2(p['dw_bias'][l]), clg=r2(p['conv_ln_g'][l]), clb=r2(p['conv_ln_b'][l]),
        wco=p['w_conv_out'][l].astype(BF16), wg=w_in[:, gate_lo:].astype(BF16), bg=b_in[None, gate_lo:],
        wmo=p['w_mla_out'][l].astype(BF16), wo=p['w_out'][l].astype(BF16),
        l1g=r2(p['ln1_g'][l]), l1b=r2(p['ln1_b'][l]), wrh=w_r_hi, wrl=w_r_lo, br=b_r,
        wegu=jnp.concatenate([p['w_gate_e'][l], p['w_up_e'][l]], axis=2).astype(BF16),
        wed=p['w_down_e'][l].astype(BF16),
        l2g=r2(p['ln2_g'][l]), l2b=r2(p['ln2_b'][l]),
    )


def _trunk(xa, xb, S, p, *, tm_proj=512, tq=2048, tk=512, tm_post=512, moe_tile=256, tm_perm=1024):
    tabs = _rope_tables(S)
    T = xa.shape[0] + xb.shape[0]
    for l in range(DEPTH):
        lw = _layer_weights(l, p)
        if l == 0:
            ln_in = (p['ln_in_g'][None, :], p['ln_in_b'][None, :])
            u, qt, k, vt, x = _proj_call((xa, xb), lw, tabs, S, tm_proj, ln_in)
        else:
            u, qt, k, vt = _proj_call((x,), lw, tabs, S, tm_proj)
        o = _attn_call(qt, k, vt, S, tq, tk)
        x1, route, cnt = _post_call(x, u, o, lw, S, tm_post)
        splits = (xa.shape[0], xb.shape[0]) if l == DEPTH - 1 else (T,)
        outs = _moe(x1, route, cnt, lw, moe_tile, tm_perm, splits)
        x = outs[0]
    return outs


def kernel(x_prompt, x_sample, ln_in_g, ln_in_b, w_in, b_in, dw_kernel, dw_bias, conv_ln_g, conv_ln_b, w_conv_out, q_norm_g, w_q_b, kv_norm_g, w_kv_b, w_mla_out, w_out, ln1_g, ln1_b, w_router_group, b_router_group, w_router_expert, b_router_expert, w_gate_e, w_up_e, w_down_e, ln2_g, ln2_b):
    p = dict(ln_in_g=ln_in_g, ln_in_b=ln_in_b, w_in=w_in, b_in=b_in, dw_kernel=dw_kernel, dw_bias=dw_bias,
             conv_ln_g=conv_ln_g, conv_ln_b=conv_ln_b, w_conv_out=w_conv_out, q_norm_g=q_norm_g, w_q_b=w_q_b,
             kv_norm_g=kv_norm_g, w_kv_b=w_kv_b, w_mla_out=w_mla_out, w_out=w_out, ln1_g=ln1_g, ln1_b=ln1_b,
             w_router_group=w_router_group, b_router_group=b_router_group,
             w_router_expert=w_router_expert, b_router_expert=b_router_expert,
             w_gate_e=w_gate_e, w_up_e=w_up_e, w_down_e=w_down_e, ln2_g=ln2_g, ln2_b=ln2_b)
    S = x_prompt.shape[1]
    assert x_sample.shape[1] == S and x_prompt.shape[2] == D_MODEL
    y_prompt, y_sample = _trunk(x_prompt.reshape(-1, D_MODEL), x_sample.reshape(-1, D_MODEL), S, p)
    return (y_prompt.reshape(x_prompt.shape), y_sample.reshape(x_sample.shape))
```

```python
import functools
import math

import jax
import jax.numpy as jnp
from jax import lax
from jax.experimental import pallas as pl
from jax.experimental.pallas import tpu as pltpu

D_MODEL = 1024
DEPTH = 2
CONV_DIM = 512
CONV_WIDTH = 31
N_HEADS = 8
QK_NOPE = 64
QK_ROPE = 32
V_HEAD = 64
Q_LORA = 384
KV_LORA = 256
ROPE_THETA = 10000.0
SOFTMAX_SCALE = (QK_NOPE + QK_ROPE) ** -0.5
N_GROUPS = 4
EXPERTS_PER_GROUP = 8
N_EXPERTS = N_GROUPS * EXPERTS_PER_GROUP
D_EXPERT = 256
DEEPNORM_ALPHA = (2 * DEPTH) ** 0.25
LN_EPS = 1e-5

LANES = 128
HEAD_PAD = LANES
HALF_ROPE = QK_ROPE // 2
V_PAD = 80
HALO = 16
GLU_COLS = 2 * CONV_DIM
SUBLANES = 8
MXU_COLS = 256
MAX_ROWS = 64
PAIRS_PER_GROUP = EXPERTS_PER_GROUP * (EXPERTS_PER_GROUP - 1) // 2
N_CLASSES = N_GROUPS * PAIRS_PER_GROUP
VMEM_LIMIT = 56 * 1024 * 1024

BF16 = jnp.bfloat16
F32 = jnp.float32


def _ln(x, g, b):
    mu = jnp.mean(x, axis=-1, keepdims=True)
    xc = x - mu
    var = jnp.mean(xc * xc, axis=-1, keepdims=True)
    return xc * lax.rsqrt(var + LN_EPS) * g + b


def _rms(x, g):
    return x * lax.rsqrt(jnp.mean(x * x, axis=-1, keepdims=True) + LN_EPS) * g


def _dot(a, b):
    return jnp.dot(a, b, preferred_element_type=F32)


def _rope(x, c, s_fwd, s_bwd):
    return x * c + pltpu.roll(x, HALF_ROPE, 1) * s_fwd + pltpu.roll(x, HEAD_PAD - HALF_ROPE, 1) * s_bwd


def _rope_t(x, c, s_fwd, s_bwd):
    return x * c + pltpu.roll(x, HALF_ROPE, 0) * s_fwd + pltpu.roll(x, HEAD_PAD - HALF_ROPE, 0) * s_bwd


def _dot_nt(a, b):
    return lax.dot_general(a, b, (((1,), (1,)), ((), ())), preferred_element_type=F32)


def _proj_kernel(x_ref, *refs):
    _proj_body(x_ref[...], *refs)


def _proj_first_kernel(xa_ref, xb_ref, g_ref, b_ref, *refs, tiles_a):
    x = jnp.where(pl.program_id(0) < tiles_a, xa_ref[...], xb_ref[...])
    x = _ln(x, g_ref[...], b_ref[...])
    refs[-1][...] = x
    _proj_body(x, *refs[:-1])


def _proj_body(x, w1_ref, b1_ref, qg_ref, wqt_ref, kvg_ref, wk_ref, wvt_ref, vone_ref,
               cq_ref, sfq_ref, sbq_ref, ck_ref, sfk_ref, sbk_ref,
               u_ref, qt_ref, k_ref, vt_ref):
    h = _dot(x.astype(BF16), w1_ref[...]) + b1_ref[...]
    a = h[:, :CONV_DIM]
    g = h[:, CONV_DIM:GLU_COLS]
    u_ref[...] = a * jax.nn.sigmoid(g)
    qa = h[:, GLU_COLS:GLU_COLS + Q_LORA]
    kva = h[:, GLU_COLS + Q_LORA:GLU_COLS + Q_LORA + KV_LORA]
    kr = h[:, GLU_COLS + Q_LORA + KV_LORA:]

    qt = _dot_nt(wqt_ref[...], _rms(qa, qg_ref[...]).astype(BF16))
    cq, sfq, sbq = cq_ref[...], sfq_ref[...], sbq_ref[...]
    for hd in range(N_HEADS):
        sl = slice(hd * HEAD_PAD, (hd + 1) * HEAD_PAD)
        qt_ref[sl, :] = _rope_t(qt[sl, :], cq, sfq, sbq).astype(BF16)

    kvn = _rms(kva, kvg_ref[...]).astype(BF16)
    kn = _dot(kvn, wk_ref[...])
    krr = _rope(kr, ck_ref[...], sfk_ref[...], sbk_ref[...])
    for hd in range(N_HEADS):
        sl = slice(hd * HEAD_PAD, (hd + 1) * HEAD_PAD)
        k_ref[:, sl] = (kn[:, sl] + krr).astype(BF16)
    vt_ref[...] = (_dot_nt(wvt_ref[...], kvn) + vone_ref[...]).astype(BF16)


def _proj_call(xs, lw, tabs, S, tm, ln_in=None):
    T = sum(x.shape[0] for x in xs)
    nper = S // tm
    row = lambda i: (i, 0)
    col = lambda i: (0, i)
    const = lambda i: (0, 0)
    full = lambda a: pl.BlockSpec(a.shape, const)
    tab_spec = pl.BlockSpec((tm, HEAD_PAD), lambda i: (i % nper, 0))
    tabt_spec = pl.BlockSpec((HEAD_PAD, tm), lambda i: (0, i % nper))
    wide = N_HEADS * HEAD_PAD
    vrows = N_HEADS * V_PAD
    weights = [lw[n] for n in ('w1', 'b1', 'qg', 'wqt', 'kvg', 'wk', 'wvt', 'vone')]
    in_specs = [full(a) for a in weights] + [tabt_spec] * 3 + [tab_spec] * 3
    out_specs = [pl.BlockSpec((tm, CONV_DIM), row), pl.BlockSpec((wide, tm), col),
                 pl.BlockSpec((tm, wide), row), pl.BlockSpec((vrows, tm), col)]
    out_shape = [jax.ShapeDtypeStruct((T, CONV_DIM), F32), jax.ShapeDtypeStruct((wide, T), BF16),
                 jax.ShapeDtypeStruct((T, wide), BF16), jax.ShapeDtypeStruct((vrows, T), BF16)]
    if ln_in is None:
        body = _proj_kernel
        in_specs = [pl.BlockSpec((tm, D_MODEL), row)] + in_specs
        args = list(xs) + weights + list(tabs)
    else:
        tiles_a = xs[0].shape[0] // tm
        body = functools.partial(_proj_first_kernel, tiles_a=tiles_a)
        in_specs = [pl.BlockSpec((tm, D_MODEL), lambda i: (jnp.minimum(i, tiles_a - 1), 0)),
                    pl.BlockSpec((tm, D_MODEL), lambda i: (jnp.maximum(i - tiles_a, 0), 0)),
                    pl.BlockSpec((1, D_MODEL), const), pl.BlockSpec((1, D_MODEL), const)] + in_specs
        out_specs = out_specs + [pl.BlockSpec((tm, D_MODEL), row)]
        out_shape = out_shape + [jax.ShapeDtypeStruct((T, D_MODEL), F32)]
        args = list(xs) + list(ln_in) + weights + list(tabs)
    return pl.pallas_call(
        body,
        grid=(T // tm,),
        in_specs=in_specs,
        out_specs=out_specs,
        out_shape=out_shape,
        compiler_params=pltpu.CompilerParams(dimension_semantics=("parallel",), vmem_limit_bytes=VMEM_LIMIT),
    )(*args)


def _attn_kernel(qt_ref, k_ref, vt_ref, o_ref, *scratch, tk):
    s_scr = (scratch[0:2], scratch[2:4])
    p_scr = (scratch[4:6], scratch[6:8])
    mx_scr, a_scr = scratch[8:10], scratch[10:12]
    m_scr, acc_scr = scratch[12:]
    n = k_ref.shape[0] // tk

    def key_rows(j):
        return pl.ds(pl.multiple_of(j * tk, tk), tk)

    pieces = [(hh, slice(None)) for hh in range(2)]

    def scores(j, slot, hh, cols):
        sl = slice(hh * HEAD_PAD, (hh + 1) * HEAD_PAD)
        s = _dot(k_ref[key_rows(j), sl], qt_ref[sl, cols])
        s_scr[slot][hh][:, cols] = s
        part = jnp.max(s.reshape(tk // MAX_ROWS, MAX_ROWS, s.shape[1]), axis=0)
        mx_scr[slot][hh, :, cols] = jnp.max(part, axis=0, keepdims=True)

    def softmax(slot, hh, cols):
        m_old = m_scr[hh, :, cols]
        m_new = jnp.maximum(m_old, mx_scr[slot][hh, :, cols])
        p_scr[slot][hh][:, cols] = jnp.exp2(s_scr[slot][hh][:, cols] - m_new).astype(BF16)
        a_scr[slot][hh, :, cols] = jnp.exp2(m_old - m_new)
        m_scr[hh, :, cols] = m_new

    def accum(j, slot, hh, cols):
        vt = vt_ref[hh * V_PAD:(hh + 1) * V_PAD, key_rows(j)]
        acc_scr[hh, :, cols] = (a_scr[slot][hh, :, cols] * acc_scr[hh, :, cols]
                                + _dot(vt, p_scr[slot][hh][:, cols]))

    def stages(score_args=None, softmax_slot=None, accum_args=None):
        for piece in pieces if score_args is not None else ():
            scores(*score_args, *piece)
        for piece in pieces if softmax_slot is not None else ():
            softmax(softmax_slot, *piece)
        for piece in pieces if accum_args is not None else ():
            accum(*accum_args, *piece)

    m_scr[...] = jnp.full(m_scr.shape, -jnp.inf, F32)
    acc_scr[...] = jnp.zeros(acc_scr.shape, F32)
    stages(score_args=(0, 0))
    stages(score_args=(1, 1), softmax_slot=0)

    def body(t, carry):
        j = 2 * t
        stages((j + 2, 0), 1, (j, 0))
        stages((j + 3, 1), 0, (j + 1, 1))
        return carry

    lax.fori_loop(0, n // 2 - 1, body, 0)
    stages(softmax_slot=1, accum_args=(n - 2, 0))
    stages(accum_args=(n - 1, 1))
    ot = jnp.concatenate([acc_scr[hh, :V_HEAD, :] / acc_scr[hh, V_HEAD:V_HEAD + 1, :] for hh in range(2)], axis=0)
    o_ref[...] = ot.T.astype(BF16)


def _attn_call(qt, k, vt, S, tq, tk):
    T = k.shape[0]
    nb = T // S
    nq = S // tq
    assert (S // tk) % 2 == 0
    return pl.pallas_call(
        functools.partial(_attn_kernel, tk=tk),
        grid=(nb, N_HEADS // 2, nq),
        in_specs=[pl.BlockSpec((2 * HEAD_PAD, tq), lambda b, hp, i: (hp, b * nq + i)),
                  pl.BlockSpec((S, 2 * HEAD_PAD), lambda b, hp, i: (b, hp)),
                  pl.BlockSpec((2 * V_PAD, S), lambda b, hp, i: (hp, b))],
        out_specs=pl.BlockSpec((tq, 2 * V_HEAD), lambda b, hp, i: (b * nq + i, hp)),
        out_shape=jax.ShapeDtypeStruct((T, N_HEADS * V_HEAD), BF16),
        scratch_shapes=[pltpu.VMEM((tk, tq), F32)] * 4 + [pltpu.VMEM((tk, tq), BF16)] * 4 + [
                        pltpu.VMEM((2, 1, tq), F32)] * 4 + [
                        pltpu.VMEM((2, 1, tq), F32),
                        pltpu.VMEM((2, V_PAD, tq), F32)],
        compiler_params=pltpu.CompilerParams(dimension_semantics=("parallel", "parallel", "parallel"),
                                             vmem_limit_bytes=VMEM_LIMIT),
    )(qt, k, vt)


def _route(logits):
    lane = lax.broadcasted_iota(jnp.int32, logits.shape, 1)
    neg = -jnp.inf
    is_group = (lane >= N_EXPERTS) & (lane < N_EXPERTS + N_GROUPS)
    gl = jnp.where(is_group, logits, neg)
    gmax = jnp.max(gl, axis=1, keepdims=True)
    g_w = 1.0 / jnp.sum(jnp.exp(gl - gmax), axis=1, keepdims=True)
    g_idx = jnp.min(jnp.where(gl == gmax, lane, 2 * LANES), axis=1, keepdims=True) - N_EXPERTS
    lo = g_idx * EXPERTS_PER_GROUP
    el = jnp.where((lane >= lo) & (lane < lo + EXPERTS_PER_GROUP), logits, neg)
    m1 = jnp.max(el, axis=1, keepdims=True)
    i1 = jnp.min(jnp.where(el == m1, lane, 2 * LANES), axis=1, keepdims=True)
    el2 = jnp.where(lane == i1, neg, el)
    m2 = jnp.max(el2, axis=1, keepdims=True)
    i2 = jnp.min(jnp.where(el2 == m2, lane, 2 * LANES), axis=1, keepdims=True)
    r = jnp.exp(m2 - m1)
    w1 = g_w / (1.0 + r)
    w2 = g_w * r / (1.0 + r)
    return g_idx, i1, i2, w1, w2, lane


def _post_kernel(x_ref, u_ref, up_ref, un_ref, o_ref,
                 dwk_ref, dwb_ref, clg_ref, clb_ref, wco_ref, wg_ref, bg_ref, wmo_ref, wo_ref,
                 l1g_ref, l1b_ref, wrh_ref, wrl_ref, br_ref,
                 x1_ref, route_ref, cnt_ref, ubuf, ush, gate_scr, mla_scr, cnt_scr, *, nper):
    tm = x_ref.shape[0]
    i = pl.program_id(0)
    first = (i % nper) == 0
    last = (i % nper) == nper - 1
    ubuf[0:HALO, :] = jnp.where(first, 0.0, up_ref[...])
    ubuf[HALO:HALO + tm, :] = u_ref[...]
    ubuf[HALO + tm:, :] = jnp.where(last, 0.0, un_ref[...])
    off = HALO - CONV_WIDTH // 2
    span = ush.shape[1]
    for b in range(SUBLANES):
        ush[b] = ubuf[off + b:off + b + span, :]

    x = x_ref[...]
    gate_scr[...] = jax.nn.sigmoid(_dot(x.astype(BF16), wg_ref[...]) + bg_ref[...])
    mla_scr[...] = _dot(o_ref[...], wmo_ref[...])

    conv = jnp.zeros((tm, CONV_DIM), F32) + dwb_ref[...]
    for b in range(SUBLANES):
        for w in range(b, CONV_WIDTH, SUBLANES):
            conv = conv + ush[b, w - b:w - b + tm, :] * dwk_ref[w:w + 1, :]
    c = _ln(conv, clg_ref[...], clb_ref[...])
    c = c * jax.nn.sigmoid(c)
    conv_out = _dot(c.astype(BF16), wco_ref[...])
    merged = gate_scr[:, :D_MODEL] * conv_out + gate_scr[:, D_MODEL:] * mla_scr[...]
    m = _dot(merged.astype(BF16), wo_ref[...])
    x1 = _ln(DEEPNORM_ALPHA * x + m, l1g_ref[...], l1b_ref[...])
    x1_ref[...] = x1

    hi = x1.astype(BF16)
    lo = (x1 - hi.astype(F32)).astype(BF16)
    logits = _dot(hi, wrh_ref[...]) + _dot(lo, wrh_ref[...]) + _dot(hi, wrl_ref[...]) + br_ref[...]
    g_idx, i1, i2, _, _, lane = _route(logits)
    la = jnp.minimum(i1, i2) - g_idx * EXPERTS_PER_GROUP
    lb = jnp.maximum(i1, i2) - g_idx * EXPERTS_PER_GROUP
    pair = jnp.right_shift(la * (2 * EXPERTS_PER_GROUP - 1 - la), 1) + (lb - la - 1)
    cls = g_idx * PAIRS_PER_GROUP + pair
    onehot = lane == cls

    @pl.when(i == 0)
    def _():
        cnt_scr[...] = jnp.zeros_like(cnt_scr)

    r_id = lax.broadcasted_iota(jnp.int32, (tm, tm), 0)
    c_id = lax.broadcasted_iota(jnp.int32, (tm, tm), 1)
    earlier = _dot((c_id < r_id).astype(BF16), onehot.astype(BF16))
    rank = jnp.sum(jnp.where(onehot, earlier + cnt_scr[...], 0.0), axis=1, keepdims=True)
    cnt_scr[...] += jnp.sum(onehot.astype(F32), axis=0, keepdims=True)
    cnt_ref[...] = cnt_scr[...]
    route_ref[...] = jnp.where(lane == 0, cls.astype(F32), jnp.where(lane == 1, rank, 0.0))


def _post_call(x, u, o, lw, S, tm):
    T = x.shape[0]
    nper = S // tm
    nh = tm // HALO
    last_halo = T // HALO - 1
    row = lambda i: (i, 0)
    const = lambda i: (0, 0)
    full = lambda a: pl.BlockSpec(a.shape, const)
    names = ['dwk', 'dwb', 'clg', 'clb', 'wco', 'wg', 'bg', 'wmo', 'wo', 'l1g', 'l1b', 'wrh', 'wrl', 'br']
    ws = [lw[n] for n in names]
    return pl.pallas_call(
        functools.partial(_post_kernel, nper=nper),
        grid=(T // tm,),
        in_specs=[pl.BlockSpec((tm, D_MODEL), row), pl.BlockSpec((tm, CONV_DIM), row),
                  pl.BlockSpec((HALO, CONV_DIM), lambda i: (jnp.maximum(i * nh - 1, 0), 0)),
                  pl.BlockSpec((HALO, CONV_DIM), lambda i: (jnp.minimum((i + 1) * nh, last_halo), 0)),
                  pl.BlockSpec((tm, N_HEADS * V_HEAD), row)] + [full(a) for a in ws],
        out_specs=[pl.BlockSpec((tm, D_MODEL), row), pl.BlockSpec((tm, LANES), row),
                   pl.BlockSpec((1, LANES), const)],
        out_shape=[jax.ShapeDtypeStruct((T, D_MODEL), F32), jax.ShapeDtypeStruct((T, LANES), F32),
                   jax.ShapeDtypeStruct((1, LANES), F32)],
        scratch_shapes=[pltpu.VMEM((tm + 2 * HALO, CONV_DIM), F32),
                        pltpu.VMEM((SUBLANES, tm + (CONV_WIDTH - 1) // SUBLANES * SUBLANES, CONV_DIM), F32),
                        pltpu.VMEM((tm, 2 * D_MODEL), F32), pltpu.VMEM((tm, D_MODEL), F32),
                        pltpu.VMEM((1, LANES), F32)],
        compiler_params=pltpu.CompilerParams(dimension_semantics=("arbitrary",), vmem_limit_bytes=VMEM_LIMIT),
    )(x, u, u, u, o, *ws)


def _class_tables(route, cnt, tile, n_tiles):
    cls = route[:, 0].astype(jnp.int32)
    rank = route[:, 1].astype(jnp.int32)
    counts = cnt[0, :N_CLASSES].astype(jnp.int32)
    tiles_of = (counts + tile - 1) // tile
    t_end = jnp.cumsum(tiles_of)
    t_start = t_end - tiles_of
    class_ids = jnp.arange(N_CLASSES, dtype=jnp.int32)[None, :]

    def lookup(idx, table):
        return jnp.sum(jnp.where(idx[:, None] == class_ids, table[None, :], 0), axis=1)

    pos = lookup(cls, t_start * tile) + rank
    g = jnp.arange(n_tiles, dtype=jnp.int32)
    blk = jnp.minimum(g, t_end[-1] - 1)
    c = jnp.sum((blk[:, None] >= t_end[None, :]).astype(jnp.int32), axis=1)
    first_row = (blk - lookup(c, t_start)) * tile
    valid = jnp.where(g < t_end[-1], jnp.clip(lookup(c, counts) - first_row, 0, tile), 0)
    pairs = [(a, b) for a in range(EXPERTS_PER_GROUP) for b in range(a + 1, EXPERTS_PER_GROUP)]
    group_lo = (class_ids[0] // PAIRS_PER_GROUP) * EXPERTS_PER_GROUP
    ea = lookup(c, group_lo + jnp.asarray([a for a, _ in pairs] * N_GROUPS, jnp.int32))
    eb = lookup(c, group_lo + jnp.asarray([b for _, b in pairs] * N_GROUPS, jnp.int32))
    return pos, blk, ea, eb, valid.astype(jnp.int32)


def _sorted_row(ref3, pos):
    return ref3.at[lax.shift_right_logical(pos, 3), pl.ds(lax.bitwise_and(pos, SUBLANES - 1), 1)]


def _scatter_kernel(pos_ref, x_ref, xs_ref, sem):
    groups = x_ref.shape[0]

    def start(i, carry):
        for u in range(SUBLANES):
            src = x_ref.at[i, pl.ds(u, 1)]
            pltpu.make_async_copy(src, _sorted_row(xs_ref, pos_ref[0, 0, i * SUBLANES + u]), sem).start()
        return carry

    lax.fori_loop(0, groups, start, 0)
    pltpu.make_async_copy(x_ref, xs_ref.at[pl.ds(0, groups)], sem).wait()


def _gather_kernel(pos_ref, xs_ref, o_ref, sem):
    groups = o_ref.shape[0]

    def start(i, carry):
        for u in range(SUBLANES):
            dst = o_ref.at[i, pl.ds(u, 1)]
            pltpu.make_async_copy(_sorted_row(xs_ref, pos_ref[0, 0, i * SUBLANES + u]), dst, sem).start()
        return carry

    lax.fori_loop(0, groups, start, 0)
    pltpu.make_async_copy(xs_ref.at[pl.ds(0, groups)], o_ref, sem).wait()


def _permute_call(kernel_fn, pos, x, n_out, tm, to_sorted):
    T = pos.shape[0]
    pos3 = pos.reshape(T // tm, 1, tm)
    pos_spec = pl.BlockSpec((1, 1, tm), lambda i: (i, 0, 0), memory_space=pltpu.SMEM)
    tile_spec = pl.BlockSpec((tm // SUBLANES, SUBLANES, D_MODEL), lambda i: (i, 0, 0))
    any_spec = pl.BlockSpec(memory_space=pl.ANY)
    out = pl.pallas_call(
        kernel_fn,
        grid=(T // tm,),
        in_specs=[pos_spec, tile_spec if to_sorted else any_spec],
        out_specs=any_spec if to_sorted else tile_spec,
        out_shape=jax.ShapeDtypeStruct((n_out // SUBLANES, SUBLANES, D_MODEL), F32),
        scratch_shapes=[pltpu.SemaphoreType.DMA],
        compiler_params=pltpu.CompilerParams(dimension_semantics=("arbitrary",), vmem_limit_bytes=VMEM_LIMIT),
    )(pos3, x.reshape(-1, SUBLANES, D_MODEL))
    return out.reshape(n_out, D_MODEL)


def _moe_kernel(blk_ref, ea_ref, eb_ref, valid_ref, xs_ref, gua_ref, da_ref, gub_ref, db_ref,
                wr_ref, br_ref, l2g_ref, l2b_ref, o_ref):
    g = pl.program_id(0)
    tile = xs_ref.shape[0]
    n_valid = valid_ref[g]

    @pl.when(n_valid > 0)
    def _():
        row = lax.broadcasted_iota(jnp.int32, (tile, 1), 0)
        x = jnp.where(row < n_valid, xs_ref[...], 0.0)
        xb = x.astype(BF16)
        logits = _dot(xb, wr_ref[...]) + br_ref[...]
        lane = lax.broadcasted_iota(jnp.int32, logits.shape, 1)
        gl = jnp.where((lane >= N_EXPERTS) & (lane < N_EXPERTS + N_GROUPS), logits, -jnp.inf)
        g_w = 1.0 / jnp.sum(jnp.exp(gl - jnp.max(gl, axis=1, keepdims=True)), axis=1, keepdims=True)
        l_a = jnp.sum(jnp.where(lane == ea_ref[g], logits, 0.0), axis=1, keepdims=True)
        l_b = jnp.sum(jnp.where(lane == eb_ref[g], logits, 0.0), axis=1, keepdims=True)
        w_a = g_w / (1.0 + jnp.exp(l_b - l_a))
        w_b = g_w / (1.0 + jnp.exp(l_a - l_b))

        def expert(gu_ref, d_ref):
            gu = _dot(xb, gu_ref[0])
            gate, up = gu[:, :D_EXPERT], gu[:, D_EXPERT:]
            return _dot((gate * jax.nn.sigmoid(gate) * up).astype(BF16), d_ref[0])

        y = w_a * expert(gua_ref, da_ref) + w_b * expert(gub_ref, db_ref)
        o_ref[...] = _ln(DEEPNORM_ALPHA * x + y, l2g_ref[...], l2b_ref[...])


def _moe_call(xs, tables, lw, tile):
    _, blk, ea, eb, valid = tables
    n_tiles = blk.shape[0]
    rows = lambda g, blk, ea, eb, valid: (blk[g], 0)
    const = lambda g, blk, ea, eb, valid: (0, 0)
    of_a = lambda g, blk, ea, eb, valid: (ea[g], 0, 0)
    of_b = lambda g, blk, ea, eb, valid: (eb[g], 0, 0)
    gu_shape, d_shape = (1, D_MODEL, 2 * D_EXPERT), (1, D_EXPERT, D_MODEL)
    grid_spec = pltpu.PrefetchScalarGridSpec(
        num_scalar_prefetch=4,
        grid=(n_tiles,),
        in_specs=[pl.BlockSpec((tile, D_MODEL), rows),
                  pl.BlockSpec(gu_shape, of_a), pl.BlockSpec(d_shape, of_a),
                  pl.BlockSpec(gu_shape, of_b), pl.BlockSpec(d_shape, of_b),
                  pl.BlockSpec((D_MODEL, LANES), const), pl.BlockSpec((1, LANES), const),
                  pl.BlockSpec((1, D_MODEL), const), pl.BlockSpec((1, D_MODEL), const)],
        out_specs=pl.BlockSpec((tile, D_MODEL), rows),
    )
    return pl.pallas_call(
        _moe_kernel,
        grid_spec=grid_spec,
        out_shape=jax.ShapeDtypeStruct(xs.shape, F32),
        compiler_params=pltpu.CompilerParams(dimension_semantics=("arbitrary",), vmem_limit_bytes=VMEM_LIMIT),
    )(blk, ea, eb, valid, xs, lw['wegu'], lw['wed'], lw['wegu'], lw['wed'], lw['wrh'], lw['br'],
      lw['l2g'], lw['l2b'])


def _moe(x1, route, cnt, lw, tile, tm, row_splits):
    T = x1.shape[0]
    n_tiles = T // tile + N_CLASSES
    tables = _class_tables(route, cnt, tile, n_tiles)
    pos = tables[0]
    xs = _permute_call(_scatter_kernel, pos, x1, n_tiles * tile, tm, True)
    ys = _moe_call(xs, tables, lw, tile)
    outs, lo = [], 0
    for n in row_splits:
        outs.append(_permute_call(_gather_kernel, pos[lo:lo + n], ys, n, tm, False))
        lo += n
    return outs


def _rope_tables(S):
    inv_freq = ROPE_THETA ** (-jnp.arange(0, QK_ROPE, 2, dtype=F32) / QK_ROPE)
    ang = jnp.arange(S, dtype=F32)[:, None] * inv_freq[None, :]
    cos, sin = jnp.cos(ang), jnp.sin(ang)
    z = lambda n: jnp.zeros((S, n), F32)
    tail = HEAD_PAD - QK_NOPE - QK_ROPE
    c = jnp.concatenate([jnp.ones((S, QK_NOPE), F32), cos, cos, z(tail)], axis=1)
    s_fwd = jnp.concatenate([z(QK_NOPE + HALF_ROPE), sin, z(tail)], axis=1)
    s_bwd = jnp.concatenate([z(QK_NOPE), -sin, z(HALF_ROPE + tail)], axis=1)
    qs = SOFTMAX_SCALE * math.log2(math.e)
    return ((c * qs).T, (s_fwd * qs).T, (s_bwd * qs).T, c, s_fwd, s_bwd)


def _layer_weights(l, p):
    w_in, b_in = p['w_in'][l], p['b_in'][l]
    rope_lo = GLU_COLS + Q_LORA + KV_LORA
    gate_lo = rope_lo + QK_ROPE
    tail = HEAD_PAD - QK_NOPE - QK_ROPE
    pad_rope = lambda a: jnp.pad(a, ((0, 0), (QK_NOPE, tail)))
    w1 = jnp.concatenate([w_in[:, :rope_lo], pad_rope(w_in[:, rope_lo:gate_lo])], axis=1)
    b1 = jnp.concatenate([b_in[None, :rope_lo], pad_rope(b_in[None, rope_lo:gate_lo])], axis=1)
    wq = jnp.pad(p['w_q_b'][l].reshape(Q_LORA, N_HEADS, QK_NOPE + QK_ROPE), ((0, 0), (0, 0), (0, tail)))
    wkv = p['w_kv_b'][l].reshape(KV_LORA, N_HEADS, QK_NOPE + V_HEAD)
    wk = jnp.pad(wkv[:, :, :QK_NOPE], ((0, 0), (0, 0), (0, HEAD_PAD - QK_NOPE)))
    wv = jnp.pad(wkv[:, :, QK_NOPE:], ((0, 0), (0, 0), (0, V_PAD - V_HEAD)))
    vone = jnp.tile(jnp.arange(V_PAD) == V_HEAD, N_HEADS).astype(F32)[:, None]
    w_r = jnp.pad(jnp.concatenate([p['w_router_expert'][l], p['w_router_group'][l]], axis=1),
                  ((0, 0), (0, LANES - N_EXPERTS - N_GROUPS)))
    b_r = jnp.pad(jnp.concatenate([p['b_router_expert'][l], p['b_router_group'][l]])[None, :],
                  ((0, 0), (0, LANES - N_EXPERTS - N_GROUPS)))
    w_r_hi = w_r.astype(BF16)
    w_r_lo = (w_r - w_r_hi.astype(F32)).astype(BF16)
    r2 = lambda a: a[None, :]
    return dict(
        w1=w1.astype(BF16), b1=b1, qg=r2(p['q_norm_g'][l]), wqt=wq.reshape(Q_LORA, -1).T.astype(BF16),
        kvg=r2(p['kv_norm_g'][l]), wk=wk.reshape(KV_LORA, -1).astype(BF16),
        wvt=wv.reshape(KV_LORA, -1).T.astype(BF16), vone=vone,
        dwk=p['dw_kernel'][l], dwb=r2(p['dw_bias'][l]), clg=r2(p['conv_ln_g'][l]), clb=r2(p['conv_ln_b'][l]),
        wco=p['w_conv_out'][l].astype(BF16), wg=w_in[:, gate_lo:].astype(BF16), bg=b_in[None, gate_lo:],
        wmo=p['w_mla_out'][l].astype(BF16), wo=p['w_out'][l].astype(BF16),
        l1g=r2(p['ln1_g'][l]), l1b=r2(p['ln1_b'][l]), wrh=w_r_hi, wrl=w_r_lo, br=b_r,
        wegu=jnp.concatenate([p['w_gate_e'][l], p['w_up_e'][l]], axis=2).astype(BF16),
        wed=p['w_down_e'][l].astype(BF16),
        l2g=r2(p['ln2_g'][l]), l2b=r2(p['ln2_b'][l]),
    )


def _trunk(xa, xb, S, p, *, tm_proj=512, tq=2048, tk=512, tm_post=512, moe_tile=256, tm_perm=4096):
    tabs = _rope_tables(S)
    T = xa.shape[0] + xb.shape[0]
    for l in range(DEPTH):
        lw = _layer_weights(l, p)
        if l == 0:
            ln_in = (p['ln_in_g'][None, :], p['ln_in_b'][None, :])
            u, qt, k, vt, x = _proj_call((xa, xb), lw, tabs, S, tm_proj, ln_in)
        else:
            u, qt, k, vt = _proj_call((x,), lw, tabs, S, tm_proj)
        o = _attn_call(qt, k, vt, S, tq, tk)
        x1, route, cnt = _post_call(x, u, o, lw, S, tm_post)
        splits = (xa.shape[0], xb.shape[0]) if l == DEPTH - 1 else (T,)
        outs = _moe(x1, route, cnt, lw, moe_tile, tm_perm, splits)
        x = outs[0]
    return outs


def kernel(x_prompt, x_sample, ln_in_g, ln_in_b, w_in, b_in, dw_kernel, dw_bias, conv_ln_g, conv_ln_b, w_conv_out, q_norm_g, w_q_b, kv_norm_g, w_kv_b, w_mla_out, w_out, ln1_g, ln1_b, w_router_group, b_router_group, w_router_expert, b_router_expert, w_gate_e, w_up_e, w_down_e, ln2_g, ln2_b):
    p = dict(ln_in_g=ln_in_g, ln_in_b=ln_in_b, w_in=w_in, b_in=b_in, dw_kernel=dw_kernel, dw_bias=dw_bias,
             conv_ln_g=conv_ln_g, conv_ln_b=conv_ln_b, w_conv_out=w_conv_out, q_norm_g=q_norm_g, w_q_b=w_q_b,
             kv_norm_g=kv_norm_g, w_kv_b=w_kv_b, w_mla_out=w_mla_out, w_out=w_out, ln1_g=ln1_g, ln1_b=ln1_b,
             w_router_group=w_router_group, b_router_group=b_router_group,
             w_router_expert=w_router_expert, b_router_expert=b_router_expert,
             w_gate_e=w_gate_e, w_up_e=w_up_e, w_down_e=w_down_e, ln2_g=ln2_g, ln2_b=ln2_b)
    S = x_prompt.shape[1]
    assert x_sample.shape[1] == S and x_prompt.shape[2] == D_MODEL
    y_prompt, y_sample = _trunk(x_prompt.reshape(-1, D_MODEL), x_sample.reshape(-1, D_MODEL), S, p)
    return (y_prompt.reshape(x_prompt.shape), y_sample.reshape(x_sample.shape))
```

```python
import functools
import math

import jax
import jax.numpy as jnp
from jax import lax
from jax.experimental import pallas as pl
from jax.experimental.pallas import tpu as pltpu

D_MODEL = 1024
DEPTH = 2
CONV_DIM = 512
CONV_WIDTH = 31
N_HEADS = 8
QK_NOPE = 64
QK_ROPE = 32
V_HEAD = 64
Q_LORA = 384
KV_LORA = 256
ROPE_THETA = 10000.0
SOFTMAX_SCALE = (QK_NOPE + QK_ROPE) ** -0.5
N_GROUPS = 4
EXPERTS_PER_GROUP = 8
N_EXPERTS = N_GROUPS * EXPERTS_PER_GROUP
D_EXPERT = 256
DEEPNORM_ALPHA = (2 * DEPTH) ** 0.25
LN_EPS = 1e-5

LANES = 128
HEAD_PAD = LANES
HALF_ROPE = QK_ROPE // 2
V_PAD = 80
HALO = 16
GLU_COLS = 2 * CONV_DIM
SUBLANES = 8
MXU_COLS = 256
MAX_ROWS = 64
PAIRS_PER_GROUP = EXPERTS_PER_GROUP * (EXPERTS_PER_GROUP - 1) // 2
N_CLASSES = N_GROUPS * PAIRS_PER_GROUP
VMEM_LIMIT = 56 * 1024 * 1024

BF16 = jnp.bfloat16
F32 = jnp.float32


def _ln(x, g, b):
    mu = jnp.mean(x, axis=-1, keepdims=True)
    xc = x - mu
    var = jnp.mean(xc * xc, axis=-1, keepdims=True)
    return xc * lax.rsqrt(var + LN_EPS) * g + b


def _rms(x, g):
    return x * lax.rsqrt(jnp.mean(x * x, axis=-1, keepdims=True) + LN_EPS) * g


def _dot(a, b):
    return jnp.dot(a, b, preferred_element_type=F32)


def _rope(x, c, s_fwd, s_bwd):
    return x * c + pltpu.roll(x, HALF_ROPE, 1) * s_fwd + pltpu.roll(x, HEAD_PAD - HALF_ROPE, 1) * s_bwd


def _rope_t(x, c, s_fwd, s_bwd):
    return x * c + pltpu.roll(x, HALF_ROPE, 0) * s_fwd + pltpu.roll(x, HEAD_PAD - HALF_ROPE, 0) * s_bwd


def _dot_nt(a, b):
    return lax.dot_general(a, b, (((1,), (1,)), ((), ())), preferred_element_type=F32)


def _proj_kernel(x_ref, *refs):
    _proj_body(x_ref[...], *refs)


def _proj_first_kernel(xa_ref, xb_ref, g_ref, b_ref, *refs, tiles_a):
    x = jnp.where(pl.program_id(0) < tiles_a, xa_ref[...], xb_ref[...])
    x = _ln(x, g_ref[...], b_ref[...])
    refs[-1][...] = x
    _proj_body(x, *refs[:-1])


def _proj_body(x, w1_ref, b1_ref, qg_ref, wqt_ref, kvg_ref, wk_ref, wvt_ref, vone_ref,
               cq_ref, sfq_ref, sbq_ref, ck_ref, sfk_ref, sbk_ref,
               u_ref, qt_ref, k_ref, vt_ref):
    h = _dot(x.astype(BF16), w1_ref[...]) + b1_ref[...]
    a = h[:, :CONV_DIM]
    g = h[:, CONV_DIM:GLU_COLS]
    u_ref[...] = a * jax.nn.sigmoid(g)
    qa = h[:, GLU_COLS:GLU_COLS + Q_LORA]
    kva = h[:, GLU_COLS + Q_LORA:GLU_COLS + Q_LORA + KV_LORA]
    kr = h[:, GLU_COLS + Q_LORA + KV_LORA:]

    qt = _dot_nt(wqt_ref[...], _rms(qa, qg_ref[...]).astype(BF16))
    cq, sfq, sbq = cq_ref[...], sfq_ref[...], sbq_ref[...]
    for hd in range(N_HEADS):
        sl = slice(hd * HEAD_PAD, (hd + 1) * HEAD_PAD)
        qt_ref[sl, :] = _rope_t(qt[sl, :], cq, sfq, sbq).astype(BF16)

    kvn = _rms(kva, kvg_ref[...]).astype(BF16)
    kn = _dot(kvn, wk_ref[...])
    krr = _rope(kr, ck_ref[...], sfk_ref[...], sbk_ref[...])
    for hd in range(N_HEADS):
        sl = slice(hd * HEAD_PAD, (hd + 1) * HEAD_PAD)
        k_ref[:, sl] = (kn[:, sl] + krr).astype(BF16)
    vt_ref[...] = (_dot_nt(wvt_ref[...], kvn) + vone_ref[...]).astype(BF16)


def _proj_call(xs, lw, tabs, S, tm, ln_in=None):
    T = sum(x.shape[0] for x in xs)
    nper = S // tm
    row = lambda i: (i, 0)
    col = lambda i: (0, i)
    const = lambda i: (0, 0)
    full = lambda a: pl.BlockSpec(a.shape, const)
    tab_spec = pl.BlockSpec((tm, HEAD_PAD), lambda i: (i % nper, 0))
    tabt_spec = pl.BlockSpec((HEAD_PAD, tm), lambda i: (0, i % nper))
    wide = N_HEADS * HEAD_PAD
    vrows = N_HEADS * V_PAD
    weights = [lw[n] for n in ('w1', 'b1', 'qg', 'wqt', 'kvg', 'wk', 'wvt', 'vone')]
    in_specs = [full(a) for a in weights] + [tabt_spec] * 3 + [tab_spec] * 3
    out_specs = [pl.BlockSpec((tm, CONV_DIM), row), pl.BlockSpec((wide, tm), col),
                 pl.BlockSpec((tm, wide), row), pl.BlockSpec((vrows, tm), col)]
    out_shape = [jax.ShapeDtypeStruct((T, CONV_DIM), F32), jax.ShapeDtypeStruct((wide, T), BF16),
                 jax.ShapeDtypeStruct((T, wide), BF16), jax.ShapeDtypeStruct((vrows, T), BF16)]
    if ln_in is None:
        body = _proj_kernel
        in_specs = [pl.BlockSpec((tm, D_MODEL), row)] + in_specs
        args = list(xs) + weights + list(tabs)
    else:
        tiles_a = xs[0].shape[0] // tm
        body = functools.partial(_proj_first_kernel, tiles_a=tiles_a)
        in_specs = [pl.BlockSpec((tm, D_MODEL), lambda i: (jnp.minimum(i, tiles_a - 1), 0)),
                    pl.BlockSpec((tm, D_MODEL), lambda i: (jnp.maximum(i - tiles_a, 0), 0)),
                    pl.BlockSpec((1, D_MODEL), const), pl.BlockSpec((1, D_MODEL), const)] + in_specs
        out_specs = out_specs + [pl.BlockSpec((tm, D_MODEL), row)]
        out_shape = out_shape + [jax.ShapeDtypeStruct((T, D_MODEL), F32)]
        args = list(xs) + list(ln_in) + weights + list(tabs)
    return pl.pallas_call(
        body,
        grid=(T // tm,),
        in_specs=in_specs,
        out_specs=out_specs,
        out_shape=out_shape,
        compiler_params=pltpu.CompilerParams(dimension_semantics=("parallel",), vmem_limit_bytes=VMEM_LIMIT),
    )(*args)


def _attn_kernel(qt_ref, k_ref, vt_ref, o_ref, *scratch, tk):
    s_scr = (scratch[0:2], scratch[2:4])
    p_scr = (scratch[4:6], scratch[6:8])
    mx_scr, a_scr = scratch[8:10], scratch[10:12]
    m_scr, acc_scr = scratch[12:]
    n = k_ref.shape[0] // tk

    def key_rows(j):
        return pl.ds(pl.multiple_of(j * tk, tk), tk)

    pieces = [(hh, slice(None)) for hh in range(2)]

    def scores(j, slot, hh, cols):
        sl = slice(hh * HEAD_PAD, (hh + 1) * HEAD_PAD)
        s = _dot(k_ref[key_rows(j), sl], qt_ref[sl, cols])
        s_scr[slot][hh][:, cols] = s
        part = jnp.max(s.reshape(tk // MAX_ROWS, MAX_ROWS, s.shape[1]), axis=0)
        mx_scr[slot][hh, :, cols] = jnp.max(part, axis=0, keepdims=True)

    def softmax(slot, hh, cols):
        m_old = m_scr[hh, :, cols]
        m_new = jnp.maximum(m_old, mx_scr[slot][hh, :, cols])
        p_scr[slot][hh][:, cols] = jnp.exp2(s_scr[slot][hh][:, cols] - m_new).astype(BF16)
        a_scr[slot][hh, :, cols] = jnp.exp2(m_old - m_new)
        m_scr[hh, :, cols] = m_new

    def accum(j, slot, hh, cols):
        vt = vt_ref[hh * V_PAD:(hh + 1) * V_PAD, key_rows(j)]
        acc_scr[hh, :, cols] = (a_scr[slot][hh, :, cols] * acc_scr[hh, :, cols]
                                + _dot(vt, p_scr[slot][hh][:, cols]))

    def stages(score_args=None, softmax_slot=None, accum_args=None):
        for piece in pieces if score_args is not None else ():
            scores(*score_args, *piece)
        for piece in pieces if softmax_slot is not None else ():
            softmax(softmax_slot, *piece)
        for piece in pieces if accum_args is not None else ():
            accum(*accum_args, *piece)

    m_scr[...] = jnp.full(m_scr.shape, -jnp.inf, F32)
    acc_scr[...] = jnp.zeros(acc_scr.shape, F32)
    stages(score_args=(0, 0))
    stages(score_args=(1, 1), softmax_slot=0)

    def body(t, carry):
        j = 2 * t
        stages((j + 2, 0), 1, (j, 0))
        stages((j + 3, 1), 0, (j + 1, 1))
        return carry

    lax.fori_loop(0, n // 2 - 1, body, 0)
    stages(softmax_slot=1, accum_args=(n - 2, 0))
    stages(accum_args=(n - 1, 1))
    ot = jnp.concatenate([acc_scr[hh, :V_HEAD, :] / acc_scr[hh, V_HEAD:V_HEAD + 1, :] for hh in range(2)], axis=0)
    o_ref[...] = ot.T.astype(BF16)


def _attn_call(qt, k, vt, S, tq, tk):
    T = k.shape[0]
    nb = T // S
    nq = S // tq
    assert (S // tk) % 2 == 0
    return pl.pallas_call(
        functools.partial(_attn_kernel, tk=tk),
        grid=(nb, N_HEADS // 2, nq),
        in_specs=[pl.BlockSpec((2 * HEAD_PAD, tq), lambda b, hp, i: (hp, b * nq + i)),
                  pl.BlockSpec((S, 2 * HEAD_PAD), lambda b, hp, i: (b, hp)),
                  pl.BlockSpec((2 * V_PAD, S), lambda b, hp, i: (hp, b))],
        out_specs=pl.BlockSpec((tq, 2 * V_HEAD), lambda b, hp, i: (b * nq + i, hp)),
        out_shape=jax.ShapeDtypeStruct((T, N_HEADS * V_HEAD), BF16),
        scratch_shapes=[pltpu.VMEM((tk, tq), F32)] * 4 + [pltpu.VMEM((tk, tq), BF16)] * 4 + [
                        pltpu.VMEM((2, 1, tq), F32)] * 4 + [
                        pltpu.VMEM((2, 1, tq), F32),
                        pltpu.VMEM((2, V_PAD, tq), F32)],
        compiler_params=pltpu.CompilerParams(dimension_semantics=("parallel", "parallel", "parallel"),
                                             vmem_limit_bytes=VMEM_LIMIT),
    )(qt, k, vt)


def _route(logits):
    lane = lax.broadcasted_iota(jnp.int32, logits.shape, 1)
    neg = -jnp.inf
    is_group = (lane >= N_EXPERTS) & (lane < N_EXPERTS + N_GROUPS)
    gl = jnp.where(is_group, logits, neg)
    gmax = jnp.max(gl, axis=1, keepdims=True)
    g_w = 1.0 / jnp.sum(jnp.exp(gl - gmax), axis=1, keepdims=True)
    g_idx = jnp.min(jnp.where(gl == gmax, lane, 2 * LANES), axis=1, keepdims=True) - N_EXPERTS
    lo = g_idx * EXPERTS_PER_GROUP
    el = jnp.where((lane >= lo) & (lane < lo + EXPERTS_PER_GROUP), logits, neg)
    m1 = jnp.max(el, axis=1, keepdims=True)
    i1 = jnp.min(jnp.where(el == m1, lane, 2 * LANES), axis=1, keepdims=True)
    el2 = jnp.where(lane == i1, neg, el)
    m2 = jnp.max(el2, axis=1, keepdims=True)
    i2 = jnp.min(jnp.where(el2 == m2, lane, 2 * LANES), axis=1, keepdims=True)
    r = jnp.exp(m2 - m1)
    w1 = g_w / (1.0 + r)
    w2 = g_w * r / (1.0 + r)
    return g_idx, i1, i2, w1, w2, lane


def _post_kernel(x_ref, u_ref, up_ref, un_ref, o_ref,
                 dwk_ref, dwb_ref, clg_ref, clb_ref, wco_ref, wg_ref, bg_ref, wmo_ref, wo_ref,
                 l1g_ref, l1b_ref, wrh_ref, wrl_ref, br_ref,
                 x1_ref, route_ref, cnt_ref, ubuf, ush, gate_scr, mla_scr, cnt_scr, *, nper):
    tm = x_ref.shape[0]
    i = pl.program_id(0)
    first = (i % nper) == 0
    last = (i % nper) == nper - 1
    ubuf[0:HALO, :] = jnp.where(first, 0.0, up_ref[...])
    ubuf[HALO:HALO + tm, :] = u_ref[...]
    ubuf[HALO + tm:, :] = jnp.where(last, 0.0, un_ref[...])
    off = HALO - CONV_WIDTH // 2
    span = ush.shape[1]
    for b in range(SUBLANES):
        ush[b] = ubuf[off + b:off + b + span, :]

    x = x_ref[...]
    gate_scr[...] = jax.nn.sigmoid(_dot(x.astype(BF16), wg_ref[...]) + bg_ref[...])
    mla_scr[...] = _dot(o_ref[...], wmo_ref[...])

    conv = jnp.zeros((tm, CONV_DIM), F32) + dwb_ref[...]
    for b in range(SUBLANES):
        for w in range(b, CONV_WIDTH, SUBLANES):
            conv = conv + ush[b, w - b:w - b + tm, :] * dwk_ref[w:w + 1, :]
    c = _ln(conv, clg_ref[...], clb_ref[...])
    c = c * jax.nn.sigmoid(c)
    conv_out = _dot(c.astype(BF16), wco_ref[...])
    merged = gate_scr[:, :D_MODEL] * conv_out + gate_scr[:, D_MODEL:] * mla_scr[...]
    m = _dot(merged.astype(BF16), wo_ref[...])
    x1 = _ln(DEEPNORM_ALPHA * x + m, l1g_ref[...], l1b_ref[...])
    x1_ref[...] = x1

    hi = x1.astype(BF16)
    lo = (x1 - hi.astype(F32)).astype(BF16)
    logits = _dot(hi, wrh_ref[...]) + _dot(lo, wrh_ref[...]) + _dot(hi, wrl_ref[...]) + br_ref[...]
    g_idx, i1, i2, _, _, lane = _route(logits)
    la = jnp.minimum(i1, i2) - g_idx * EXPERTS_PER_GROUP
    lb = jnp.maximum(i1, i2) - g_idx * EXPERTS_PER_GROUP
    pair = jnp.right_shift(la * (2 * EXPERTS_PER_GROUP - 1 - la), 1) + (lb - la - 1)
    cls = g_idx * PAIRS_PER_GROUP + pair
    onehot = lane == cls

    @pl.when(i == 0)
    def _():
        cnt_scr[...] = jnp.zeros_like(cnt_scr)

    r_id = lax.broadcasted_iota(jnp.int32, (tm, tm), 0)
    c_id = lax.broadcasted_iota(jnp.int32, (tm, tm), 1)
    earlier = _dot((c_id < r_id).astype(BF16), onehot.astype(BF16))
    rank = jnp.sum(jnp.where(onehot, earlier + cnt_scr[...], 0.0), axis=1, keepdims=True)
    cnt_scr[...] += jnp.sum(onehot.astype(F32), axis=0, keepdims=True)
    cnt_ref[...] = cnt_scr[...]
    route_ref[...] = jnp.where(lane == 0, cls.astype(F32), jnp.where(lane == 1, rank, 0.0))


def _post_call(x, u, o, lw, S, tm):
    T = x.shape[0]
    nper = S // tm
    nh = tm // HALO
    last_halo = T // HALO - 1
    row = lambda i: (i, 0)
    const = lambda i: (0, 0)
    full = lambda a: pl.BlockSpec(a.shape, const)
    names = ['dwk', 'dwb', 'clg', 'clb', 'wco', 'wg', 'bg', 'wmo', 'wo', 'l1g', 'l1b', 'wrh', 'wrl', 'br']
    ws = [lw[n] for n in names]
    return pl.pallas_call(
        functools.partial(_post_kernel, nper=nper),
        grid=(T // tm,),
        in_specs=[pl.BlockSpec((tm, D_MODEL), row), pl.BlockSpec((tm, CONV_DIM), row),
                  pl.BlockSpec((HALO, CONV_DIM), lambda i: (jnp.maximum(i * nh - 1, 0), 0)),
                  pl.BlockSpec((HALO, CONV_DIM), lambda i: (jnp.minimum((i + 1) * nh, last_halo), 0)),
                  pl.BlockSpec((tm, N_HEADS * V_HEAD), row)] + [full(a) for a in ws],
        out_specs=[pl.BlockSpec((tm, D_MODEL), row), pl.BlockSpec((tm, LANES), row),
                   pl.BlockSpec((1, LANES), const)],
        out_shape=[jax.ShapeDtypeStruct((T, D_MODEL), F32), jax.ShapeDtypeStruct((T, LANES), F32),
                   jax.ShapeDtypeStruct((1, LANES), F32)],
        scratch_shapes=[pltpu.VMEM((tm + 2 * HALO, CONV_DIM), F32),
                        pltpu.VMEM((SUBLANES, tm + (CONV_WIDTH - 1) // SUBLANES * SUBLANES, CONV_DIM), F32),
                        pltpu.VMEM((tm, 2 * D_MODEL), F32), pltpu.VMEM((tm, D_MODEL), F32),
                        pltpu.VMEM((1, LANES), F32)],
        compiler_params=pltpu.CompilerParams(dimension_semantics=("arbitrary",), vmem_limit_bytes=VMEM_LIMIT),
    )(x, u, u, u, o, *ws)


def _class_tables(route, cnt, tile, n_tiles):
    cls = route[:, 0].astype(jnp.int32)
    rank = route[:, 1].astype(jnp.int32)
    counts = cnt[0, :N_CLASSES].astype(jnp.int32)
    tiles_of = (counts + tile - 1) // tile
    t_end = jnp.cumsum(tiles_of)
    t_start = t_end - tiles_of
    class_ids = jnp.arange(N_CLASSES, dtype=jnp.int32)[None, :]

    def lookup(idx, table):
        return jnp.sum(jnp.where(idx[:, None] == class_ids, table[None, :], 0), axis=1)

    pos = lookup(cls, t_start * tile) + rank
    g = jnp.arange(n_tiles, dtype=jnp.int32)
    blk = jnp.minimum(g, t_end[-1] - 1)
    c = jnp.sum((blk[:, None] >= t_end[None, :]).astype(jnp.int32), axis=1)
    first_row = (blk - lookup(c, t_start)) * tile
    valid = jnp.where(g < t_end[-1], jnp.clip(lookup(c, counts) - first_row, 0, tile), 0)
    pairs = [(a, b) for a in range(EXPERTS_PER_GROUP) for b in range(a + 1, EXPERTS_PER_GROUP)]
    group_lo = (class_ids[0] // PAIRS_PER_GROUP) * EXPERTS_PER_GROUP
    ea = lookup(c, group_lo + jnp.asarray([a for a, _ in pairs] * N_GROUPS, jnp.int32))
    eb = lookup(c, group_lo + jnp.asarray([b for _, b in pairs] * N_GROUPS, jnp.int32))
    return pos, blk, ea, eb, valid.astype(jnp.int32)


def _sorted_row(ref3, pos):
    return ref3.at[lax.shift_right_logical(pos, 3), pl.ds(lax.bitwise_and(pos, SUBLANES - 1), 1)]


def _scatter_kernel(pos_ref, x_ref, xs_ref, sem):
    groups = x_ref.shape[0]

    def start(i, carry):
        for u in range(SUBLANES):
            src = x_ref.at[i, pl.ds(u, 1)]
            pltpu.make_async_copy(src, _sorted_row(xs_ref, pos_ref[0, 0, i * SUBLANES + u]), sem).start()
        return carry

    lax.fori_loop(0, groups, start, 0)
    pltpu.make_async_copy(x_ref, xs_ref.at[pl.ds(0, groups)], sem).wait()


def _gather_kernel(pos_ref, xs_ref, o_ref, sem):
    groups = o_ref.shape[0]

    def start(i, carry):
        for u in range(SUBLANES):
            dst = o_ref.at[i, pl.ds(u, 1)]
            pltpu.make_async_copy(_sorted_row(xs_ref, pos_ref[0, 0, i * SUBLANES + u]), dst, sem).start()
        return carry

    lax.fori_loop(0, groups, start, 0)
    pltpu.make_async_copy(xs_ref.at[pl.ds(0, groups)], o_ref, sem).wait()


def _permute_call(kernel_fn, pos, x, n_out, tm, to_sorted):
    T = pos.shape[0]
    pos3 = pos.reshape(T // tm, 1, tm)
    pos_spec = pl.BlockSpec((1, 1, tm), lambda i: (i, 0, 0), memory_space=pltpu.SMEM)
    tile_spec = pl.BlockSpec((tm // SUBLANES, SUBLANES, D_MODEL), lambda i: (i, 0, 0))
    any_spec = pl.BlockSpec(memory_space=pl.ANY)
    out = pl.pallas_call(
        kernel_fn,
        grid=(T // tm,),
        in_specs=[pos_spec, tile_spec if to_sorted else any_spec],
        out_specs=any_spec if to_sorted else tile_spec,
        out_shape=jax.ShapeDtypeStruct((n_out // SUBLANES, SUBLANES, D_MODEL), F32),
        scratch_shapes=[pltpu.SemaphoreType.DMA],
        compiler_params=pltpu.CompilerParams(dimension_semantics=("arbitrary",), vmem_limit_bytes=VMEM_LIMIT),
    )(pos3, x.reshape(-1, SUBLANES, D_MODEL))
    return out.reshape(n_out, D_MODEL)


def _moe_kernel(blk_ref, ea_ref, eb_ref, valid_ref, xs_ref, gua_ref, da_ref, gub_ref, db_ref,
                wr_ref, br_ref, l2g_ref, l2b_ref, o_ref):
    g = pl.program_id(0)
    tile = xs_ref.shape[0]
    n_valid = valid_ref[g]

    @pl.when(n_valid > 0)
    def _():
        row = lax.broadcasted_iota(jnp.int32, (tile, 1), 0)
        x = jnp.where(row < n_valid, xs_ref[...], 0.0)
        xb = x.astype(BF16)
        logits = _dot(xb, wr_ref[...]) + br_ref[...]
        lane = lax.broadcasted_iota(jnp.int32, logits.shape, 1)
        gl = jnp.where((lane >= N_EXPERTS) & (lane < N_EXPERTS + N_GROUPS), logits, -jnp.inf)
        g_w = 1.0 / jnp.sum(jnp.exp(gl - jnp.max(gl, axis=1, keepdims=True)), axis=1, keepdims=True)
        l_a = jnp.sum(jnp.where(lane == ea_ref[g], logits, 0.0), axis=1, keepdims=True)
        l_b = jnp.sum(jnp.where(lane == eb_ref[g], logits, 0.0), axis=1, keepdims=True)
        w_a = g_w / (1.0 + jnp.exp(l_b - l_a))
        w_b = g_w / (1.0 + jnp.exp(l_a - l_b))

        def expert(gu_ref, d_ref):
            gu = _dot(xb, gu_ref[0])
            gate, up = gu[:, :D_EXPERT], gu[:, D_EXPERT:]
            return _dot((gate * jax.nn.sigmoid(gate) * up).astype(BF16), d_ref[0])

        y = w_a * expert(gua_ref, da_ref) + w_b * expert(gub_ref, db_ref)
        o_ref[...] = _ln(DEEPNORM_ALPHA * x + y, l2g_ref[...], l2b_ref[...])


def _moe_call(xs, tables, lw, tile):
    _, blk, ea, eb, valid = tables
    n_tiles = blk.shape[0]
    rows = lambda g, blk, ea, eb, valid: (blk[g], 0)
    const = lambda g, blk, ea, eb, valid: (0, 0)
    of_a = lambda g, blk, ea, eb, valid: (ea[g], 0, 0)
    of_b = lambda g, blk, ea, eb, valid: (eb[g], 0, 0)
    gu_shape, d_shape = (1, D_MODEL, 2 * D_EXPERT), (1, D_EXPERT, D_MODEL)
    grid_spec = pltpu.PrefetchScalarGridSpec(
        num_scalar_prefetch=4,
        grid=(n_tiles,),
        in_specs=[pl.BlockSpec((tile, D_MODEL), rows),
                  pl.BlockSpec(gu_shape, of_a), pl.BlockSpec(d_shape, of_a),
                  pl.BlockSpec(gu_shape, of_b), pl.BlockSpec(d_shape, of_b),
                  pl.BlockSpec((D_MODEL, LANES), const), pl.BlockSpec((1, LANES), const),
                  pl.BlockSpec((1, D_MODEL), const), pl.BlockSpec((1, D_MODEL), const)],
        out_specs=pl.BlockSpec((tile, D_MODEL), rows),
    )
    return pl.pallas_call(
        _moe_kernel,
        grid_spec=grid_spec,
        out_shape=jax.ShapeDtypeStruct(xs.shape, F32),
        compiler_params=pltpu.CompilerParams(dimension_semantics=("arbitrary",), vmem_limit_bytes=VMEM_LIMIT),
    )(blk, ea, eb, valid, xs, lw['wegu'], lw['wed'], lw['wegu'], lw['wed'], lw['wrh'], lw['br'],
      lw['l2g'], lw['l2b'])


def _moe(x1, route, cnt, lw, tile, tm, row_splits):
    T = x1.shape[0]
    n_tiles = T // tile + N_CLASSES
    tables = _class_tables(route, cnt, tile, n_tiles)
    pos = tables[0]
    xs = _permute_call(_scatter_kernel, pos, x1, n_tiles * tile, tm, True)
    ys = _moe_call(xs, tables, lw, tile)
    outs, lo = [], 0
    for n in row_splits:
        outs.append(_permute_call(_gather_kernel, pos[lo:lo + n], ys, n, tm, False))
        lo += n
    return outs


def _rope_tables(S):
    inv_freq = ROPE_THETA ** (-jnp.arange(0, QK_ROPE, 2, dtype=F32) / QK_ROPE)
    ang = jnp.arange(S, dtype=F32)[:, None] * inv_freq[None, :]
    cos, sin = jnp.cos(ang), jnp.sin(ang)
    z = lambda n: jnp.zeros((S, n), F32)
    tail = HEAD_PAD - QK_NOPE - QK_ROPE
    c = jnp.concatenate([jnp.ones((S, QK_NOPE), F32), cos, cos, z(tail)], axis=1)
    s_fwd = jnp.concatenate([z(QK_NOPE + HALF_ROPE), sin, z(tail)], axis=1)
    s_bwd = jnp.concatenate([z(QK_NOPE), -sin, z(HALF_ROPE + tail)], axis=1)
    qs = SOFTMAX_SCALE * math.log2(math.e)
    return ((c * qs).T, (s_fwd * qs).T, (s_bwd * qs).T, c, s_fwd, s_bwd)


def _layer_weights(l, p):
    w_in, b_in = p['w_in'][l], p['b_in'][l]
    rope_lo = GLU_COLS + Q_LORA + KV_LORA
    gate_lo = rope_lo + QK_ROPE
    tail = HEAD_PAD - QK_NOPE - QK_ROPE
    pad_rope = lambda a: jnp.pad(a, ((0, 0), (QK_NOPE, tail)))
    w1 = jnp.concatenate([w_in[:, :rope_lo], pad_rope(w_in[:, rope_lo:gate_lo])], axis=1)
    b1 = jnp.concatenate([b_in[None, :rope_lo], pad_rope(b_in[None, rope_lo:gate_lo])], axis=1)
    wq = jnp.pad(p['w_q_b'][l].reshape(Q_LORA, N_HEADS, QK_NOPE + QK_ROPE), ((0, 0), (0, 0), (0, tail)))
    wkv = p['w_kv_b'][l].reshape(KV_LORA, N_HEADS, QK_NOPE + V_HEAD)
    wk = jnp.pad(wkv[:, :, :QK_NOPE], ((0, 0), (0, 0), (0, HEAD_PAD - QK_NOPE)))
    wv = jnp.pad(wkv[:, :, QK_NOPE:], ((0, 0), (0, 0), (0, V_PAD - V_HEAD)))
    vone = jnp.tile(jnp.arange(V_PAD) == V_HEAD, N_HEADS).astype(F32)[:, None]
    w_r = jnp.pad(jnp.concatenate([p['w_router_expert'][l], p['w_router_group'][l]], axis=1),
                  ((0, 0), (0, LANES - N_EXPERTS - N_GROUPS)))
    b_r = jnp.pad(jnp.concatenate([p['b_router_expert'][l], p['b_router_group'][l]])[None, :],
                  ((0, 0), (0, LANES - N_EXPERTS - N_GROUPS)))
    w_r_hi = w_r.astype(BF16)
    w_r_lo = (w_r - w_r_hi.astype(F32)).astype(BF16)
    r2 = lambda a: a[None, :]
    return dict(
        w1=w1.astype(BF16), b1=b1, qg=r2(p['q_norm_g'][l]), wqt=wq.reshape(Q_LORA, -1).T.astype(BF16),
        kvg=r2(p['kv_norm_g'][l]), wk=wk.reshape(KV_LORA, -1).astype(BF16),
        wvt=wv.reshape(KV_LORA, -1).T.astype(BF16), vone=vone,
        dwk=p['dw_kernel'][l], dwb=r2(p['dw_bias'][l]), clg=r2(p['conv_ln_g'][l]), clb=r2(p['conv_ln_b'][l]),
        wco=p['w_conv_out'][l].astype(BF16), wg=w_in[:, gate_lo:].astype(BF16), bg=b_in[None, gate_lo:],
        wmo=p['w_mla_out'][l].astype(BF16), wo=p['w_out'][l].astype(BF16),
        l1g=r2(p['ln1_g'][l]), l1b=r2(p['ln1_b'][l]), wrh=w_r_hi, wrl=w_r_lo, br=b_r,
        wegu=jnp.concatenate([p['w_gate_e'][l], p['w_up_e'][l]], axis=2).astype(BF16),
        wed=p['w_down_e'][l].astype(BF16),
        l2g=r2(p['ln2_g'][l]), l2b=r2(p['ln2_b'][l]),
    )


def _trunk(xa, xb, S, p, *, tm_proj=1024, tq=2048, tk=512, tm_post=512, moe_tile=256, tm_perm=4096):
    tabs = _rope_tables(S)
    T = xa.shape[0] + xb.shape[0]
    for l in range(DEPTH):
        lw = _layer_weights(l, p)
        if l == 0:
            ln_in = (p['ln_in_g'][None, :], p['ln_in_b'][None, :])
            u, qt, k, vt, x = _proj_call((xa, xb), lw, tabs, S, tm_proj, ln_in)
        else:
            u, qt, k, vt = _proj_call((x,), lw, tabs, S, tm_proj)
        o = _attn_call(qt, k, vt, S, tq, tk)
        x1, route, cnt = _post_call(x, u, o, lw, S, tm_post)
        splits = (xa.shape[0], xb.shape[0]) if l == DEPTH - 1 else (T,)
        outs = _moe(x1, route, cnt, lw, moe_tile, tm_perm, splits)
        x = outs[0]
    return outs


def kernel(x_prompt, x_sample, ln_in_g, ln_in_b, w_in, b_in, dw_kernel, dw_bias, conv_ln_g, conv_ln_b, w_conv_out, q_norm_g, w_q_b, kv_norm_g, w_kv_b, w_mla_out, w_out, ln1_g, ln1_b, w_router_group, b_router_group, w_router_expert, b_router_expert, w_gate_e, w_up_e, w_down_e, ln2_g, ln2_b):
    p = dict(ln_in_g=ln_in_g, ln_in_b=ln_in_b, w_in=w_in, b_in=b_in, dw_kernel=dw_kernel, dw_bias=dw_bias,
             conv_ln_g=conv_ln_g, conv_ln_b=conv_ln_b, w_conv_out=w_conv_out, q_norm_g=q_norm_g, w_q_b=w_q_b,
             kv_norm_g=kv_norm_g, w_kv_b=w_kv_b, w_mla_out=w_mla_out, w_out=w_out, ln1_g=ln1_g, ln1_b=ln1_b,
             w_router_group=w_router_group, b_router_group=b_router_group,
             w_router_expert=w_router_expert, b_router_expert=b_router_expert,
             w_gate_e=w_gate_e, w_up_e=w_up_e, w_down_e=w_down_e, ln2_g=ln2_g, ln2_b=ln2_b)
    S = x_prompt.shape[1]
    assert x_sample.shape[1] == S and x_prompt.shape[2] == D_MODEL
    y_prompt, y_sample = _trunk(x_prompt.reshape(-1, D_MODEL), x_sample.reshape(-1, D_MODEL), S, p)
    return (y_prompt.reshape(x_prompt.shape), y_sample.reshape(x_sample.shape))
```

```python
import functools
import math

import jax
import jax.numpy as jnp
from jax import lax
from jax.experimental import pallas as pl
from jax.experimental.pallas import tpu as pltpu

D_MODEL = 1024
DEPTH = 2
CONV_DIM = 512
CONV_WIDTH = 31
N_HEADS = 8
QK_NOPE = 64
QK_ROPE = 32
V_HEAD = 64
Q_LORA = 384
KV_LORA = 256
ROPE_THETA = 10000.0
SOFTMAX_SCALE = (QK_NOPE + QK_ROPE) ** -0.5
N_GROUPS = 4
EXPERTS_PER_GROUP = 8
N_EXPERTS = N_GROUPS * EXPERTS_PER_GROUP
D_EXPERT = 256
DEEPNORM_ALPHA = (2 * DEPTH) ** 0.25
LN_EPS = 1e-5

LANES = 128
HEAD_PAD = LANES
HALF_ROPE = QK_ROPE // 2
V_PAD = 80
HALO = 16
GLU_COLS = 2 * CONV_DIM
SUBLANES = 8
MXU_COLS = 256
MAX_ROWS = 64
PAIRS_PER_GROUP = EXPERTS_PER_GROUP * (EXPERTS_PER_GROUP - 1) // 2
N_CLASSES = N_GROUPS * PAIRS_PER_GROUP
VMEM_LIMIT = 56 * 1024 * 1024

BF16 = jnp.bfloat16
F32 = jnp.float32


def _ln(x, g, b):
    mu = jnp.mean(x, axis=-1, keepdims=True)
    xc = x - mu
    var = jnp.mean(xc * xc, axis=-1, keepdims=True)
    return xc * lax.rsqrt(var + LN_EPS) * g + b


def _rms(x, g):
    return x * lax.rsqrt(jnp.mean(x * x, axis=-1, keepdims=True) + LN_EPS) * g


def _dot(a, b):
    return jnp.dot(a, b, preferred_element_type=F32)


def _rope(x, c, s_fwd, s_bwd):
    return x * c + pltpu.roll(x, HALF_ROPE, 1) * s_fwd + pltpu.roll(x, HEAD_PAD - HALF_ROPE, 1) * s_bwd


def _rope_t(x, c, s_fwd, s_bwd):
    return x * c + pltpu.roll(x, HALF_ROPE, 0) * s_fwd + pltpu.roll(x, HEAD_PAD - HALF_ROPE, 0) * s_bwd


def _dot_nt(a, b):
    return lax.dot_general(a, b, (((1,), (1,)), ((), ())), preferred_element_type=F32)


def _proj_kernel(x_ref, *refs):
    _proj_body(x_ref[...], *refs)


def _proj_first_kernel(xa_ref, xb_ref, g_ref, b_ref, *refs, tiles_a):
    x = jnp.where(pl.program_id(0) < tiles_a, xa_ref[...], xb_ref[...])
    x = _ln(x, g_ref[...], b_ref[...])
    refs[-1][...] = x
    _proj_body(x, *refs[:-1])


def _proj_body(x, w1_ref, b1_ref, qg_ref, wqt_ref, kvg_ref, wk_ref, wvt_ref, vone_ref,
               cq_ref, sfq_ref, sbq_ref, ck_ref, sfk_ref, sbk_ref,
               u_ref, qt_ref, k_ref, vt_ref):
    h = _dot(x.astype(BF16), w1_ref[...]) + b1_ref[...]
    a = h[:, :CONV_DIM]
    g = h[:, CONV_DIM:GLU_COLS]
    u_ref[...] = a * jax.nn.sigmoid(g)
    qa = h[:, GLU_COLS:GLU_COLS + Q_LORA]
    kva = h[:, GLU_COLS + Q_LORA:GLU_COLS + Q_LORA + KV_LORA]
    kr = h[:, GLU_COLS + Q_LORA + KV_LORA:]

    qt = _dot_nt(wqt_ref[...], _rms(qa, qg_ref[...]).astype(BF16))
    cq, sfq, sbq = cq_ref[...], sfq_ref[...], sbq_ref[...]
    for hd in range(N_HEADS):
        sl = slice(hd * HEAD_PAD, (hd + 1) * HEAD_PAD)
        qt_ref[sl, :] = _rope_t(qt[sl, :], cq, sfq, sbq).astype(BF16)

    kvn = _rms(kva, kvg_ref[...]).astype(BF16)
    kn = _dot(kvn, wk_ref[...])
    krr = _rope(kr, ck_ref[...], sfk_ref[...], sbk_ref[...])
    for hd in range(N_HEADS):
        sl = slice(hd * HEAD_PAD, (hd + 1) * HEAD_PAD)
        k_ref[:, sl] = (kn[:, sl] + krr).astype(BF16)
    vt_ref[...] = (_dot_nt(wvt_ref[...], kvn) + vone_ref[...]).astype(BF16)


def _proj_call(xs, lw, tabs, S, tm, ln_in=None):
    T = sum(x.shape[0] for x in xs)
    nper = S // tm
    row = lambda i: (i, 0)
    col = lambda i: (0, i)
    const = lambda i: (0, 0)
    full = lambda a: pl.BlockSpec(a.shape, const)
    tab_spec = pl.BlockSpec((tm, HEAD_PAD), lambda i: (i % nper, 0))
    tabt_spec = pl.BlockSpec((HEAD_PAD, tm), lambda i: (0, i % nper))
    wide = N_HEADS * HEAD_PAD
    vrows = N_HEADS * V_PAD
    weights = [lw[n] for n in ('w1', 'b1', 'qg', 'wqt', 'kvg', 'wk', 'wvt', 'vone')]
    in_specs = [full(a) for a in weights] + [tabt_spec] * 3 + [tab_spec] * 3
    out_specs = [pl.BlockSpec((tm, CONV_DIM), row), pl.BlockSpec((wide, tm), col),
                 pl.BlockSpec((tm, wide), row), pl.BlockSpec((vrows, tm), col)]
    out_shape = [jax.ShapeDtypeStruct((T, CONV_DIM), F32), jax.ShapeDtypeStruct((wide, T), BF16),
                 jax.ShapeDtypeStruct((T, wide), BF16), jax.ShapeDtypeStruct((vrows, T), BF16)]
    if ln_in is None:
        body = _proj_kernel
        in_specs = [pl.BlockSpec((tm, D_MODEL), row)] + in_specs
        args = list(xs) + weights + list(tabs)
    else:
        tiles_a = xs[0].shape[0] // tm
        body = functools.partial(_proj_first_kernel, tiles_a=tiles_a)
        in_specs = [pl.BlockSpec((tm, D_MODEL), lambda i: (jnp.minimum(i, tiles_a - 1), 0)),
                    pl.BlockSpec((tm, D_MODEL), lambda i: (jnp.maximum(i - tiles_a, 0), 0)),
                    pl.BlockSpec((1, D_MODEL), const), pl.BlockSpec((1, D_MODEL), const)] + in_specs
        out_specs = out_specs + [pl.BlockSpec((tm, D_MODEL), row)]
        out_shape = out_shape + [jax.ShapeDtypeStruct((T, D_MODEL), F32)]
        args = list(xs) + list(ln_in) + weights + list(tabs)
    return pl.pallas_call(
        body,
        grid=(T // tm,),
        in_specs=in_specs,
        out_specs=out_specs,
        out_shape=out_shape,
        compiler_params=pltpu.CompilerParams(dimension_semantics=("parallel",), vmem_limit_bytes=VMEM_LIMIT),
    )(*args)


def _attn_kernel(qt_ref, k_ref, vt_ref, o_ref, *scratch, tk):
    s_scr = (scratch[0:2], scratch[2:4])
    p_scr = (scratch[4:6], scratch[6:8])
    mx_scr, a_scr = scratch[8:10], scratch[10:12]
    m_scr, acc_scr = scratch[12:]
    n = k_ref.shape[0] // tk

    def key_rows(j):
        return pl.ds(pl.multiple_of(j * tk, tk), tk)

    pieces = [(hh, slice(None)) for hh in range(2)]

    def scores(j, slot, hh, cols):
        sl = slice(hh * HEAD_PAD, (hh + 1) * HEAD_PAD)
        s = _dot(k_ref[key_rows(j), sl], qt_ref[sl, cols])
        s_scr[slot][hh][:, cols] = s
        part = jnp.max(s.reshape(tk // MAX_ROWS, MAX_ROWS, s.shape[1]), axis=0)
        mx_scr[slot][hh, :, cols] = jnp.max(part, axis=0, keepdims=True)

    def softmax(slot, hh, cols):
        m_old = m_scr[hh, :, cols]
        m_new = jnp.maximum(m_old, mx_scr[slot][hh, :, cols])
        p_scr[slot][hh][:, cols] = jnp.exp2(s_scr[slot][hh][:, cols] - m_new).astype(BF16)
        a_scr[slot][hh, :, cols] = jnp.exp2(m_old - m_new)
        m_scr[hh, :, cols] = m_new

    def accum(j, slot, hh, cols):
        vt = vt_ref[hh * V_PAD:(hh + 1) * V_PAD, key_rows(j)]
        acc_scr[hh, :, cols] = (a_scr[slot][hh, :, cols] * acc_scr[hh, :, cols]
                                + _dot(vt, p_scr[slot][hh][:, cols]))

    def stages(score_args=None, softmax_slot=None, accum_args=None):
        for piece in pieces if score_args is not None else ():
            scores(*score_args, *piece)
        for piece in pieces if softmax_slot is not None else ():
            softmax(softmax_slot, *piece)
        for piece in pieces if accum_args is not None else ():
            accum(*accum_args, *piece)

    m_scr[...] = jnp.full(m_scr.shape, -jnp.inf, F32)
    acc_scr[...] = jnp.zeros(acc_scr.shape, F32)
    stages(score_args=(0, 0))
    stages(score_args=(1, 1), softmax_slot=0)

    def body(t, carry):
        j = 2 * t
        stages((j + 2, 0), 1, (j, 0))
        stages((j + 3, 1), 0, (j + 1, 1))
        return carry

    lax.fori_loop(0, n // 2 - 1, body, 0)
    stages(softmax_slot=1, accum_args=(n - 2, 0))
    stages(accum_args=(n - 1, 1))
    ot = jnp.concatenate([acc_scr[hh, :V_HEAD, :] / acc_scr[hh, V_HEAD:V_HEAD + 1, :] for hh in range(2)], axis=0)
    o_ref[...] = ot.T.astype(BF16)


def _attn_call(qt, k, vt, S, tq, tk):
    T = k.shape[0]
    nb = T // S
    nq = S // tq
    assert (S // tk) % 2 == 0
    return pl.pallas_call(
        functools.partial(_attn_kernel, tk=tk),
        grid=(nb, N_HEADS // 2, nq),
        in_specs=[pl.BlockSpec((2 * HEAD_PAD, tq), lambda b, hp, i: (hp, b * nq + i)),
                  pl.BlockSpec((S, 2 * HEAD_PAD), lambda b, hp, i: (b, hp)),
                  pl.BlockSpec((2 * V_PAD, S), lambda b, hp, i: (hp, b))],
        out_specs=pl.BlockSpec((tq, 2 * V_HEAD), lambda b, hp, i: (b * nq + i, hp)),
        out_shape=jax.ShapeDtypeStruct((T, N_HEADS * V_HEAD), BF16),
        scratch_shapes=[pltpu.VMEM((tk, tq), F32)] * 4 + [pltpu.VMEM((tk, tq), BF16)] * 4 + [
                        pltpu.VMEM((2, 1, tq), F32)] * 4 + [
                        pltpu.VMEM((2, 1, tq), F32),
                        pltpu.VMEM((2, V_PAD, tq), F32)],
        compiler_params=pltpu.CompilerParams(dimension_semantics=("parallel", "parallel", "parallel"),
                                             vmem_limit_bytes=VMEM_LIMIT),
    )(qt, k, vt)


def _route(logits):
    lane = lax.broadcasted_iota(jnp.int32, logits.shape, 1)
    neg = -jnp.inf
    is_group = (lane >= N_EXPERTS) & (lane < N_EXPERTS + N_GROUPS)
    gl = jnp.where(is_group, logits, neg)
    gmax = jnp.max(gl, axis=1, keepdims=True)
    g_w = 1.0 / jnp.sum(jnp.exp(gl - gmax), axis=1, keepdims=True)
    g_idx = jnp.min(jnp.where(gl == gmax, lane, 2 * LANES), axis=1, keepdims=True) - N_EXPERTS
    lo = g_idx * EXPERTS_PER_GROUP
    el = jnp.where((lane >= lo) & (lane < lo + EXPERTS_PER_GROUP), logits, neg)
    m1 = jnp.max(el, axis=1, keepdims=True)
    i1 = jnp.min(jnp.where(el == m1, lane, 2 * LANES), axis=1, keepdims=True)
    el2 = jnp.where(lane == i1, neg, el)
    m2 = jnp.max(el2, axis=1, keepdims=True)
    i2 = jnp.min(jnp.where(el2 == m2, lane, 2 * LANES), axis=1, keepdims=True)
    r = jnp.exp(m2 - m1)
    w1 = g_w / (1.0 + r)
    w2 = g_w * r / (1.0 + r)
    return g_idx, i1, i2, w1, w2, lane


def _post_kernel(x_ref, u_ref, up_ref, un_ref, o_ref,
                 dwk_ref, dwb_ref, clg_ref, clb_ref, wco_ref, wg_ref, bg_ref, wmo_ref, wo_ref,
                 l1g_ref, l1b_ref, wrh_ref, wrl_ref, br_ref,
                 x1_ref, route_ref, cnt_ref, ubuf, ush, gate_scr, mla_scr, cnt_scr, *, nper):
    tm = x_ref.shape[0]
    i = pl.program_id(0)
    first = (i % nper) == 0
    last = (i % nper) == nper - 1
    ubuf[0:HALO, :] = jnp.where(first, 0.0, up_ref[...])
    ubuf[HALO:HALO + tm, :] = u_ref[...]
    ubuf[HALO + tm:, :] = jnp.where(last, 0.0, un_ref[...])
    off = HALO - CONV_WIDTH // 2
    span = ush.shape[1]
    for b in range(SUBLANES):
        ush[b] = ubuf[off + b:off + b + span, :]

    x = x_ref[...]
    gate_scr[...] = jax.nn.sigmoid(_dot(x.astype(BF16), wg_ref[...]) + bg_ref[...])
    mla_scr[...] = _dot(o_ref[...], wmo_ref[...])

    conv = jnp.zeros((tm, CONV_DIM), F32) + dwb_ref[...]
    for b in range(SUBLANES):
        for w in range(b, CONV_WIDTH, SUBLANES):
            conv = conv + ush[b, w - b:w - b + tm, :] * dwk_ref[w:w + 1, :]
    c = _ln(conv, clg_ref[...], clb_ref[...])
    c = c * jax.nn.sigmoid(c)
    conv_out = _dot(c.astype(BF16), wco_ref[...])
    merged = gate_scr[:, :D_MODEL] * conv_out + gate_scr[:, D_MODEL:] * mla_scr[...]
    m = _dot(merged.astype(BF16), wo_ref[...])
    x1 = _ln(DEEPNORM_ALPHA * x + m, l1g_ref[...], l1b_ref[...])
    x1_ref[...] = x1

    hi = x1.astype(BF16)
    lo = (x1 - hi.astype(F32)).astype(BF16)
    logits = _dot(hi, wrh_ref[...]) + _dot(lo, wrh_ref[...]) + _dot(hi, wrl_ref[...]) + br_ref[...]
    g_idx, i1, i2, _, _, lane = _route(logits)
    la = jnp.minimum(i1, i2) - g_idx * EXPERTS_PER_GROUP
    lb = jnp.maximum(i1, i2) - g_idx * EXPERTS_PER_GROUP
    pair = jnp.right_shift(la * (2 * EXPERTS_PER_GROUP - 1 - la), 1) + (lb - la - 1)
    cls = g_idx * PAIRS_PER_GROUP + pair
    onehot = lane == cls

    @pl.when(i == 0)
    def _():
        cnt_scr[...] = jnp.zeros_like(cnt_scr)

    r_id = lax.broadcasted_iota(jnp.int32, (tm, tm), 0)
    c_id = lax.broadcasted_iota(jnp.int32, (tm, tm), 1)
    earlier = _dot((c_id < r_id).astype(BF16), onehot.astype(BF16))
    rank = jnp.sum(jnp.where(onehot, earlier + cnt_scr[...], 0.0), axis=1, keepdims=True)
    cnt_scr[...] += jnp.sum(onehot.astype(F32), axis=0, keepdims=True)
    cnt_ref[...] = cnt_scr[...]
    route_ref[...] = jnp.where(lane == 0, cls.astype(F32), jnp.where(lane == 1, rank, 0.0))


def _post_call(x, u, o, lw, S, tm):
    T = x.shape[0]
    nper = S // tm
    nh = tm // HALO
    last_halo = T // HALO - 1
    row = lambda i: (i, 0)
    const = lambda i: (0, 0)
    full = lambda a: pl.BlockSpec(a.shape, const)
    names = ['dwk', 'dwb', 'clg', 'clb', 'wco', 'wg', 'bg', 'wmo', 'wo', 'l1g', 'l1b', 'wrh', 'wrl', 'br']
    ws = [lw[n] for n in names]
    return pl.pallas_call(
        functools.partial(_post_kernel, nper=nper),
        grid=(T // tm,),
        in_specs=[pl.BlockSpec((tm, D_MODEL), row), pl.BlockSpec((tm, CONV_DIM), row),
                  pl.BlockSpec((HALO, CONV_DIM), lambda i: (jnp.maximum(i * nh - 1, 0), 0)),
                  pl.BlockSpec((HALO, CONV_DIM), lambda i: (jnp.minimum((i + 1) * nh, last_halo), 0)),
                  pl.BlockSpec((tm, N_HEADS * V_HEAD), row)] + [full(a) for a in ws],
        out_specs=[pl.BlockSpec((tm, D_MODEL), row), pl.BlockSpec((tm, LANES), row),
                   pl.BlockSpec((1, LANES), const)],
        out_shape=[jax.ShapeDtypeStruct((T, D_MODEL), F32), jax.ShapeDtypeStruct((T, LANES), F32),
                   jax.ShapeDtypeStruct((1, LANES), F32)],
        scratch_shapes=[pltpu.VMEM((tm + 2 * HALO, CONV_DIM), F32),
                        pltpu.VMEM((SUBLANES, tm + (CONV_WIDTH - 1) // SUBLANES * SUBLANES, CONV_DIM), F32),
                        pltpu.VMEM((tm, 2 * D_MODEL), F32), pltpu.VMEM((tm, D_MODEL), F32),
                        pltpu.VMEM((1, LANES), F32)],
        compiler_params=pltpu.CompilerParams(dimension_semantics=("arbitrary",), vmem_limit_bytes=VMEM_LIMIT),
    )(x, u, u, u, o, *ws)


def _class_tables(route, cnt, tile, n_tiles):
    cls = route[:, 0].astype(jnp.int32)
    rank = route[:, 1].astype(jnp.int32)
    counts = cnt[0, :N_CLASSES].astype(jnp.int32)
    tiles_of = (counts + tile - 1) // tile
    t_end = jnp.cumsum(tiles_of)
    t_start = t_end - tiles_of
    class_ids = jnp.arange(N_CLASSES, dtype=jnp.int32)[None, :]

    def lookup(idx, table):
        return jnp.sum(jnp.where(idx[:, None] == class_ids, table[None, :], 0), axis=1)

    pos = lookup(cls, t_start * tile) + rank
    g = jnp.arange(n_tiles, dtype=jnp.int32)
    blk = jnp.minimum(g, t_end[-1] - 1)
    c = jnp.sum((blk[:, None] >= t_end[None, :]).astype(jnp.int32), axis=1)
    first_row = (blk - lookup(c, t_start)) * tile
    valid = jnp.where(g < t_end[-1], jnp.clip(lookup(c, counts) - first_row, 0, tile), 0)
    pairs = [(a, b) for a in range(EXPERTS_PER_GROUP) for b in range(a + 1, EXPERTS_PER_GROUP)]
    group_lo = (class_ids[0] // PAIRS_PER_GROUP) * EXPERTS_PER_GROUP
    ea = lookup(c, group_lo + jnp.asarray([a for a, _ in pairs] * N_GROUPS, jnp.int32))
    eb = lookup(c, group_lo + jnp.asarray([b for _, b in pairs] * N_GROUPS, jnp.int32))
    return pos, blk, ea, eb, valid.astype(jnp.int32)


def _sorted_row(ref3, pos):
    return ref3.at[lax.shift_right_logical(pos, 3), pl.ds(lax.bitwise_and(pos, SUBLANES - 1), 1)]


def _scatter_kernel(pos_ref, x_ref, xs_ref, sem):
    groups = x_ref.shape[0]

    def start(i, carry):
        for u in range(SUBLANES):
            src = x_ref.at[i, pl.ds(u, 1)]
            pltpu.make_async_copy(src, _sorted_row(xs_ref, pos_ref[0, 0, i * SUBLANES + u]), sem).start()
        return carry

    lax.fori_loop(0, groups, start, 0)
    pltpu.make_async_copy(x_ref, xs_ref.at[pl.ds(0, groups)], sem).wait()


def _gather_kernel(pos_ref, xs_ref, o_ref, sem):
    groups = o_ref.shape[0]

    def start(i, carry):
        for u in range(SUBLANES):
            dst = o_ref.at[i, pl.ds(u, 1)]
            pltpu.make_async_copy(_sorted_row(xs_ref, pos_ref[0, 0, i * SUBLANES + u]), dst, sem).start()
        return carry

    lax.fori_loop(0, groups, start, 0)
    pltpu.make_async_copy(xs_ref.at[pl.ds(0, groups)], o_ref, sem).wait()


def _permute_call(kernel_fn, pos, x, n_out, tm, to_sorted):
    T = pos.shape[0]
    pos3 = pos.reshape(T // tm, 1, tm)
    pos_spec = pl.BlockSpec((1, 1, tm), lambda i: (i, 0, 0), memory_space=pltpu.SMEM)
    tile_spec = pl.BlockSpec((tm // SUBLANES, SUBLANES, D_MODEL), lambda i: (i, 0, 0))
    any_spec = pl.BlockSpec(memory_space=pl.ANY)
    out = pl.pallas_call(
        kernel_fn,
        grid=(T // tm,),
        in_specs=[pos_spec, tile_spec if to_sorted else any_spec],
        out_specs=any_spec if to_sorted else tile_spec,
        out_shape=jax.ShapeDtypeStruct((n_out // SUBLANES, SUBLANES, D_MODEL), F32),
        scratch_shapes=[pltpu.SemaphoreType.DMA],
        compiler_params=pltpu.CompilerParams(dimension_semantics=("arbitrary",), vmem_limit_bytes=VMEM_LIMIT),
    )(pos3, x.reshape(-1, SUBLANES, D_MODEL))
    return out.reshape(n_out, D_MODEL)


def _moe_kernel(blk_ref, ea_ref, eb_ref, valid_ref, xs_ref, gua_ref, da_ref, gub_ref, db_ref,
                wr_ref, br_ref, l2g_ref, l2b_ref, o_ref):
    g = pl.program_id(0)
    tile = xs_ref.shape[0]
    n_valid = valid_ref[g]

    @pl.when(n_valid > 0)
    def _():
        row = lax.broadcasted_iota(jnp.int32, (tile, 1), 0)
        x = jnp.where(row < n_valid, xs_ref[...], 0.0)
        xb = x.astype(BF16)
        logits = _dot(xb, wr_ref[...]) + br_ref[...]
        lane = lax.broadcasted_iota(jnp.int32, logits.shape, 1)
        gl = jnp.where((lane >= N_EXPERTS) & (lane < N_EXPERTS + N_GROUPS), logits, -jnp.inf)
        g_w = 1.0 / jnp.sum(jnp.exp(gl - jnp.max(gl, axis=1, keepdims=True)), axis=1, keepdims=True)
        l_a = jnp.sum(jnp.where(lane == ea_ref[g], logits, 0.0), axis=1, keepdims=True)
        l_b = jnp.sum(jnp.where(lane == eb_ref[g], logits, 0.0), axis=1, keepdims=True)
        w_a = g_w / (1.0 + jnp.exp(l_b - l_a))
        w_b = g_w / (1.0 + jnp.exp(l_a - l_b))

        def expert(gu_ref, d_ref):
            gu = _dot(xb, gu_ref[0])
            gate, up = gu[:, :D_EXPERT], gu[:, D_EXPERT:]
            return _dot((gate * jax.nn.sigmoid(gate) * up).astype(BF16), d_ref[0])

        y = w_a * expert(gua_ref, da_ref) + w_b * expert(gub_ref, db_ref)
        o_ref[...] = _ln(DEEPNORM_ALPHA * x + y, l2g_ref[...], l2b_ref[...])


def _moe_call(xs, tables, lw, tile):
    _, blk, ea, eb, valid = tables
    n_tiles = blk.shape[0]
    rows = lambda g, blk, ea, eb, valid: (blk[g], 0)
    const = lambda g, blk, ea, eb, valid: (0, 0)
    of_a = lambda g, blk, ea, eb, valid: (ea[g], 0, 0)
    of_b = lambda g, blk, ea, eb, valid: (eb[g], 0, 0)
    gu_shape, d_shape = (1, D_MODEL, 2 * D_EXPERT), (1, D_EXPERT, D_MODEL)
    grid_spec = pltpu.PrefetchScalarGridSpec(
        num_scalar_prefetch=4,
        grid=(n_tiles,),
        in_specs=[pl.BlockSpec((tile, D_MODEL), rows),
                  pl.BlockSpec(gu_shape, of_a), pl.BlockSpec(d_shape, of_a),
                  pl.BlockSpec(gu_shape, of_b), pl.BlockSpec(d_shape, of_b),
                  pl.BlockSpec((D_MODEL, LANES), const), pl.BlockSpec((1, LANES), const),
                  pl.BlockSpec((1, D_MODEL), const), pl.BlockSpec((1, D_MODEL), const)],
        out_specs=pl.BlockSpec((tile, D_MODEL), rows),
    )
    return pl.pallas_call(
        _moe_kernel,
        grid_spec=grid_spec,
        out_shape=jax.ShapeDtypeStruct(xs.shape, F32),
        compiler_params=pltpu.CompilerParams(dimension_semantics=("arbitrary",), vmem_limit_bytes=VMEM_LIMIT),
    )(blk, ea, eb, valid, xs, lw['wegu'], lw['wed'], lw['wegu'], lw['wed'], lw['wrh'], lw['br'],
      lw['l2g'], lw['l2b'])


def _moe(x1, route, cnt, lw, tile, tm, row_splits):
    T = x1.shape[0]
    n_tiles = T // tile + N_CLASSES
    tables = _class_tables(route, cnt, tile, n_tiles)
    pos = tables[0]
    xs = _permute_call(_scatter_kernel, pos, x1, n_tiles * tile, tm, True)
    ys = _moe_call(xs, tables, lw, tile)
    outs, lo = [], 0
    for n in row_splits:
        outs.append(_permute_call(_gather_kernel, pos[lo:lo + n], ys, n, tm, False))
        lo += n
    return outs


def _rope_tables(S):
    inv_freq = ROPE_THETA ** (-jnp.arange(0, QK_ROPE, 2, dtype=F32) / QK_ROPE)
    ang = jnp.arange(S, dtype=F32)[:, None] * inv_freq[None, :]
    cos, sin = jnp.cos(ang), jnp.sin(ang)
    z = lambda n: jnp.zeros((S, n), F32)
    tail = HEAD_PAD - QK_NOPE - QK_ROPE
    c = jnp.concatenate([jnp.ones((S, QK_NOPE), F32), cos, cos, z(tail)], axis=1)
    s_fwd = jnp.concatenate([z(QK_NOPE + HALF_ROPE), sin, z(tail)], axis=1)
    s_bwd = jnp.concatenate([z(QK_NOPE), -sin, z(HALF_ROPE + tail)], axis=1)
    qs = SOFTMAX_SCALE * math.log2(math.e)
    return ((c * qs).T, (s_fwd * qs).T, (s_bwd * qs).T, c, s_fwd, s_bwd)


def _layer_weights(l, p):
    w_in, b_in = p['w_in'][l], p['b_in'][l]
    rope_lo = GLU_COLS + Q_LORA + KV_LORA
    gate_lo = rope_lo + QK_ROPE
    tail = HEAD_PAD - QK_NOPE - QK_ROPE
    pad_rope = lambda a: jnp.pad(a, ((0, 0), (QK_NOPE, tail)))
    w1 = jnp.concatenate([w_in[:, :rope_lo], pad_rope(w_in[:, rope_lo:gate_lo])], axis=1)
    b1 = jnp.concatenate([b_in[None, :rope_lo], pad_rope(b_in[None, rope_lo:gate_lo])], axis=1)
    wq = jnp.pad(p['w_q_b'][l].reshape(Q_LORA, N_HEADS, QK_NOPE + QK_ROPE), ((0, 0), (0, 0), (0, tail)))
    wkv = p['w_kv_b'][l].reshape(KV_LORA, N_HEADS, QK_NOPE + V_HEAD)
    wk = jnp.pad(wkv[:, :, :QK_NOPE], ((0, 0), (0, 0), (0, HEAD_PAD - QK_NOPE)))
    wv = jnp.pad(wkv[:, :, QK_NOPE:], ((0, 0), (0, 0), (0, V_PAD - V_HEAD)))
    vone = jnp.tile(jnp.arange(V_PAD) == V_HEAD, N_HEADS).astype(F32)[:, None]
    w_r = jnp.pad(jnp.concatenate([p['w_router_expert'][l], p['w_router_group'][l]], axis=1),
                  ((0, 0), (0, LANES - N_EXPERTS - N_GROUPS)))
    b_r = jnp.pad(jnp.concatenate([p['b_router_expert'][l], p['b_router_group'][l]])[None, :],
                  ((0, 0), (0, LANES - N_EXPERTS - N_GROUPS)))
    w_r_hi = w_r.astype(BF16)
    w_r_lo = (w_r - w_r_hi.astype(F32)).astype(BF16)
    r2 = lambda a: a[None, :]
    return dict(
        w1=w1.astype(BF16), b1=b1, qg=r2(p['q_norm_g'][l]), wqt=wq.reshape(Q_LORA, -1).T.astype(BF16),
        kvg=r2(p['kv_norm_g'][l]), wk=wk.reshape(KV_LORA, -1).astype(BF16),
        wvt=wv.reshape(KV_LORA, -1).T.astype(BF16), vone=vone,
        dwk=p['dw_kernel'][l], dwb=r2(p['dw_bias'][l]), clg=r2(p['conv_ln_g'][l]), clb=r2(p['conv_ln_b'][l]),
        wco=p['w_conv_out'][l].astype(BF16), wg=w_in[:, gate_lo:].astype(BF16), bg=b_in[None, gate_lo:],
        wmo=p['w_mla_out'][l].astype(BF16), wo=p['w_out'][l].astype(BF16),
        l1g=r2(p['ln1_g'][l]), l1b=r2(p['ln1_b'][l]), wrh=w_r_hi, wrl=w_r_lo, br=b_r,
        wegu=jnp.concatenate([p['w_gate_e'][l], p['w_up_e'][l]], axis=2).astype(BF16),
        wed=p['w_down_e'][l].astype(BF16),
        l2g=r2(p['ln2_g'][l]), l2b=r2(p['ln2_b'][l]),
    )


def _trunk(xa, xb, S, p, *, tm_proj=1024, tq=2048, tk=512, tm_post=512, moe_tile=512, tm_perm=4096):
    tabs = _rope_tables(S)
    T = xa.shape[0] + xb.shape[0]
    for l in range(DEPTH):
        lw = _layer_weights(l, p)
        if l == 0:
            ln_in = (p['ln_in_g'][None, :], p['ln_in_b'][None, :])
            u, qt, k, vt, x = _proj_call((xa, xb), lw, tabs, S, tm_proj, ln_in)
        else:
            u, qt, k, vt = _proj_call((x,), lw, tabs, S, tm_proj)
        o = _attn_call(qt, k, vt, S, tq, tk)
        x1, route, cnt = _post_call(x, u, o, lw, S, tm_post)
        splits = (xa.shape[0], xb.shape[0]) if l == DEPTH - 1 else (T,)
        outs = _moe(x1, route, cnt, lw, moe_tile, tm_perm, splits)
        x = outs[0]
    return outs


def kernel(x_prompt, x_sample, ln_in_g, ln_in_b, w_in, b_in, dw_kernel, dw_bias, conv_ln_g, conv_ln_b, w_conv_out, q_norm_g, w_q_b, kv_norm_g, w_kv_b, w_mla_out, w_out, ln1_g, ln1_b, w_router_group, b_router_group, w_router_expert, b_router_expert, w_gate_e, w_up_e, w_down_e, ln2_g, ln2_b):
    p = dict(ln_in_g=ln_in_g, ln_in_b=ln_in_b, w_in=w_in, b_in=b_in, dw_kernel=dw_kernel, dw_bias=dw_bias,
             conv_ln_g=conv_ln_g, conv_ln_b=conv_ln_b, w_conv_out=w_conv_out, q_norm_g=q_norm_g, w_q_b=w_q_b,
             kv_norm_g=kv_norm_g, w_kv_b=w_kv_b, w_mla_out=w_mla_out, w_out=w_out, ln1_g=ln1_g, ln1_b=ln1_b,
             w_router_group=w_router_group, b_router_group=b_router_group,
             w_router_expert=w_router_expert, b_router_expert=b_router_expert,
             w_gate_e=w_gate_e, w_up_e=w_up_e, w_down_e=w_down_e, ln2_g=ln2_g, ln2_b=ln2_b)
    S = x_prompt.shape[1]
    assert x_sample.shape[1] == S and x_prompt.shape[2] == D_MODEL
    y_prompt, y_sample = _trunk(x_prompt.reshape(-1, D_MODEL), x_sample.reshape(-1, D_MODEL), S, p)
    return (y_prompt.reshape(x_prompt.shape), y_sample.reshape(x_sample.shape))
```
